```python
import functools
import math
import jax, jax.numpy as jnp
from jax import lax
import numpy as np

D_MODEL = 2048
BATCH = 32
SEQ = 256
DEPTH = 4
DEC_BATCH = 4
DEC_SEQ = 4096
PAST_LEN = 256

GRID_W = 64
BLOCK = 128
EPS = 1e-6
NEG_INF = -1e30
N_Q_HEADS = 16
N_KV_HEADS = 4
HEAD_DIM = 64
Q_PER_KV = N_Q_HEADS // N_KV_HEADS
WINDOW = 128
ROPE_BASE = 10000.0
ROPE_FREQS = HEAD_DIM // 4
ATTN_SCALE = HEAD_DIM ** -0.5
ATTN_WIDTH = N_Q_HEADS * HEAD_DIM
KV_WIDTH = N_KV_HEADS * HEAD_DIM
CONV_WIDTH = 1024
CONV_KERNEL = 31
SSD_HEADS = 16
SSD_HEAD_DIM = 64
SSD_INNER = SSD_HEADS * SSD_HEAD_DIM
SSD_GROUPS = 2
SSD_HEADS_PER_GROUP = SSD_HEADS // SSD_GROUPS
D_STATE = 128
SSD_CONV = 4
SSD_CHUNK = 128
XBC_WIDTH = SSD_INNER + 2 * SSD_GROUPS * D_STATE
PEER_HEADS = 8
N_KEYS = 128
N_EXPERTS = N_KEYS * N_KEYS
PEER_KEY_DIM = 256
PEER_HALF = PEER_KEY_DIM // 2
PEER_TOPK = 16
PEER_SLOTS = PEER_HEADS * PEER_TOPK
PEER_TOKEN_BLOCK = 128
N_BRANCH = 3
IN_SPLITS = (ATTN_WIDTH, KV_WIDTH, KV_WIDTH, 2 * CONV_WIDTH, SSD_INNER, XBC_WIDTH, 2 * SSD_HEADS)
IN_WIDTH = ATTN_WIDTH + 2 * KV_WIDTH + 2 * CONV_WIDTH + SSD_INNER + XBC_WIDTH + 2 * SSD_HEADS + N_BRANCH * D_MODEL

kernel_name = 'hybrid_diffusion_prefix_trunk_step'


def rms_norm(x, g):
    xf = x.astype(jnp.float32)
    y = xf * lax.rsqrt(jnp.mean(xf * xf, axis=-1, keepdims=True) + EPS)
    return (y * g.astype(jnp.float32)).astype(x.dtype)


def layer_norm(x, g, b):
    xf = x.astype(jnp.float32)
    mu = jnp.mean(xf, axis=-1, keepdims=True)
    var = jnp.mean(jnp.square(xf - mu), axis=-1, keepdims=True)
    y = (xf - mu) * lax.rsqrt(var + EPS) * g.astype(jnp.float32) + b.astype(jnp.float32)
    return y.astype(x.dtype)


def modulation(cond, w, b):
    m = jax.nn.silu(cond) @ w + b
    return m.reshape(cond.shape[:-1] + (6, D_MODEL))


def split_in(u):
    offsets = []
    acc = 0
    for w in IN_SPLITS:
        acc += w
        offsets.append(acc)
    return jnp.split(u, offsets, axis=-1)


def grid_rope_angles(length):
    rows = length // GRID_W
    row = jnp.repeat(jnp.arange(rows, dtype=jnp.float32), GRID_W)
    col = jnp.tile(jnp.arange(GRID_W, dtype=jnp.float32), rows)
    freqs = ROPE_BASE ** (-jnp.arange(ROPE_FREQS, dtype=jnp.float32) / ROPE_FREQS)
    return row[:, None] * freqs, col[:, None] * freqs


def rotate_pairs(x, ang):
    cos = jnp.cos(ang)[None, :, None, :].astype(x.dtype)
    sin = jnp.sin(ang)[None, :, None, :].astype(x.dtype)
    x1, x2 = x[..., :ROPE_FREQS], x[..., ROPE_FREQS:]
    return jnp.concatenate([x1 * cos - x2 * sin, x2 * cos + x1 * sin], axis=-1)


def axial_rope(x, ang_row, ang_col):
    half = HEAD_DIM // 2
    return jnp.concatenate([rotate_pairs(x[..., :half], ang_row),
                            rotate_pairs(x[..., half:], ang_col)], axis=-1)


def sink_softmax(s, sink):
    sk = jnp.broadcast_to(sink.astype(jnp.float32).reshape(N_KV_HEADS, Q_PER_KV, 1, 1), s.shape[:-1] + (1,))
    return jax.nn.softmax(jnp.concatenate([s, sk], axis=-1), axis=-1)[..., :-1]


def ctx_attention(q, k, v, sink):
    bsz, length = q.shape[:2]
    nb = length // BLOCK
    qb = jnp.moveaxis(q.reshape(bsz, nb, BLOCK, N_KV_HEADS, Q_PER_KV, HEAD_DIM), 1, 0)

    def one_block(qblk):
        s = jnp.einsum('bqkgd,bskd->bkgqs', qblk, k).astype(jnp.float32) * ATTN_SCALE
        p = sink_softmax(s, sink).astype(v.dtype)
        return jnp.einsum('bkgqs,bskd->bqkgd', p, v)

    o = lax.map(one_block, qb)
    return jnp.moveaxis(o, 0, 1).reshape(bsz, length, ATTN_WIDTH)


def latent_attention(q, k, v, sink, ck, cv, ang_row, ang_col):
    bsz, length = q.shape[:2]
    nb = length // BLOCK
    q = axial_rope(q, ang_row, ang_col)
    k = axial_rope(k, ang_row, ang_col)
    qb = jnp.moveaxis(q.reshape(bsz, nb, BLOCK, N_KV_HEADS, Q_PER_KV, HEAD_DIM), 1, 0)

    def neighbours(t):
        tp = jnp.pad(t, ((0, 0), (BLOCK, BLOCK), (0, 0), (0, 0)))
        tp = tp.reshape(bsz, nb + 2, BLOCK, N_KV_HEADS, HEAD_DIM)
        return jnp.moveaxis(jnp.concatenate([tp[:, :-2], tp[:, 1:-1], tp[:, 2:]], axis=2), 1, 0)

    kw, vw = neighbours(k), neighbours(v)
    n_win = 3 * BLOCK

    def one_block(args):
        qblk, kblk, vblk, n = args
        qi = n * BLOCK + jnp.arange(BLOCK)
        kj = n * BLOCK - BLOCK + jnp.arange(n_win)
        mask = (jnp.abs(qi[:, None] - kj[None, :]) <= WINDOW) & (kj >= 0)[None, :] & (kj < length)[None, :]
        s_win = jnp.einsum('bqkgd,bskd->bkgqs', qblk, kblk).astype(jnp.float32) * ATTN_SCALE
        s_win = jnp.where(mask, s_win, NEG_INF)
        s_ctx = jnp.einsum('bqkgd,bckd->bkgqc', qblk, ck).astype(jnp.float32) * ATTN_SCALE
        p = sink_softmax(jnp.concatenate([s_win, s_ctx], axis=-1), sink).astype(vblk.dtype)
        return (jnp.einsum('bkgqs,bskd->bqkgd', p[..., :n_win], vblk)
                + jnp.einsum('bkgqc,bckd->bqkgd', p[..., n_win:], cv))

    o = lax.map(one_block, (qb, kw, vw, jnp.arange(nb)))
    return jnp.moveaxis(o, 0, 1).reshape(bsz, length, ATTN_WIDTH)


def conformer_conv(u, dw_w, dw_b, ln_g, ln_b):
    a, g = jnp.split(u, 2, axis=-1)
    h = a * jax.nn.sigmoid(g)
    pad = CONV_KERNEL // 2
    h = lax.conv_general_dilated(h, dw_w[:, None, :], (1,), [(pad, pad)],
                                 dimension_numbers=('NWC', 'WIO', 'NWC'),
                                 feature_group_count=CONV_WIDTH) + dw_b
    return jax.nn.silu(layer_norm(h, ln_g, ln_b))


def causal_dwconv(u, w, b):
    return lax.conv_general_dilated(u, w[:, None, :], (1,), [(SSD_CONV - 1, 0)],
                                    dimension_numbers=('NWC', 'WIO', 'NWC'),
                                    feature_group_count=XBC_WIDTH) + b


def ssd_chunked(x, dt, A, Bm, Cm, h0):
    bsz, length = x.shape[:2]
    nc = length // SSD_CHUNK
    shp = (bsz, nc, SSD_CHUNK, SSD_GROUPS, SSD_HEADS_PER_GROUP)
    xg = (x * dt[..., None]).reshape(shp + (SSD_HEAD_DIM,))
    a_cum = jnp.cumsum((dt * A).reshape(shp), axis=2)
    Bc = Bm.reshape(bsz, nc, SSD_CHUNK, SSD_GROUPS, D_STATE)
    Cc = Cm.reshape(bsz, nc, SSD_CHUNK, SSD_GROUPS, D_STATE)
    lower = jnp.tril(jnp.ones((SSD_CHUNK, SSD_CHUNK), dtype=bool))[:, :, None, None]
    seg = a_cum[:, :, :, None] - a_cum[:, :, None, :]
    decay = jnp.exp(jnp.where(lower, seg, -jnp.inf))
    cb = jnp.einsum('bctgn,bcsgn->bctsg', Cc, Bc)
    y_diag = jnp.einsum('bctsgh,bcsghp->bctghp', cb[..., None] * decay, xg)
    to_end = jnp.exp(a_cum[:, :, -1:] - a_cum)
    chunk_states = jnp.einsum('bclgn,bclghp->bcghpn', Bc, xg * to_end[..., None])
    chunk_decay = jnp.exp(a_cum[:, :, -1])

    def step(h, inp):
        st, dc = inp
        return dc[..., None, None] * h + st, h

    h_init = h0.reshape(bsz, SSD_GROUPS, SSD_HEADS_PER_GROUP, SSD_HEAD_DIM, D_STATE)
    h_final, h_in = lax.scan(step, h_init, (jnp.moveaxis(chunk_states, 1, 0), jnp.moveaxis(chunk_decay, 1, 0)))
    h_in = jnp.moveaxis(h_in, 0, 1)
    y_off = jnp.einsum('bclgn,bcghpn->bclghp', Cc, h_in) * jnp.exp(a_cum)[..., None]
    y = (y_diag + y_off).reshape(bsz, length, SSD_HEADS, SSD_HEAD_DIM)
    return y, h_final.reshape(bsz, SSD_HEADS, SSD_HEAD_DIM, D_STATE)


def maybe_flip(t, direction):
    return jnp.flip(t, axis=1) if direction == 1 else t


def ssd_bidirectional(xbc, dt_raw, z, conv_w, conv_b, A_log, dt_bias, D_skip, norm_g, h0):
    bsz, length = xbc.shape[:2]
    ys = []
    finals = []
    for d in range(2):
        u = jax.nn.silu(causal_dwconv(maybe_flip(xbc, d), conv_w[d], conv_b[d])).astype(jnp.float32)
        xs, Bm, Cm = jnp.split(u, [SSD_INNER, SSD_INNER + SSD_GROUPS * D_STATE], axis=-1)
        xs = xs.reshape(bsz, length, SSD_HEADS, SSD_HEAD_DIM)
        Bm = Bm.reshape(bsz, length, SSD_GROUPS, D_STATE)
        Cm = Cm.reshape(bsz, length, SSD_GROUPS, D_STATE)
        dt = jax.nn.softplus(maybe_flip(dt_raw[..., d * SSD_HEADS:(d + 1) * SSD_HEADS], d).astype(jnp.float32)
                             + dt_bias[d].astype(jnp.float32))
        A = -jnp.exp(A_log[d].astype(jnp.float32))
        y, hf = ssd_chunked(xs, dt, A, Bm, Cm, h0[:, d].astype(jnp.float32))
        y = y + D_skip[d].astype(jnp.float32)[:, None] * xs
        ys.append(maybe_flip(y, d))
        finals.append(hf)
    y = (ys[0] + ys[1]).reshape(bsz, length, SSD_INNER)
    y = rms_norm(y * jax.nn.silu(z.astype(jnp.float32)), norm_g).astype(z.dtype)
    return y, jnp.stack(finals, axis=1)


def peer(h, w_q, sub_keys, U, V):
    bsz, length = h.shape[:2]
    T = bsz * length
    hf = h.reshape(T, D_MODEL)
    q = (hf @ w_q).reshape(T, PEER_HEADS, 2, PEER_HALF)
    s = jnp.einsum('thcd,hcnd->thcn', q, sub_keys).astype(jnp.float32)
    s1, i1 = lax.top_k(s[:, :, 0], PEER_TOPK)
    s2, i2 = lax.top_k(s[:, :, 1], PEER_TOPK)
    cand = (s1[..., :, None] + s2[..., None, :]).reshape(T, PEER_HEADS, PEER_TOPK * PEER_TOPK)
    cidx = (i1[..., :, None] * N_KEYS + i2[..., None, :]).reshape(T, PEER_HEADS, PEER_TOPK * PEER_TOPK)
    top, pos = lax.top_k(cand, PEER_TOPK)
    idx = jnp.take_along_axis(cidx, pos, axis=-1)
    w = jax.nn.softmax(top, axis=-1)
    nblk = T // PEER_TOKEN_BLOCK

    def expert_block(args):
        xb, ib, wb = args
        a = jax.nn.gelu(jnp.einsum('td,ted->te', xb, jnp.take(U, ib, axis=0)).astype(jnp.float32), approximate=False)
        return jnp.einsum('te,ted->td', (wb * a).astype(xb.dtype), jnp.take(V, ib, axis=0))

    out = lax.map(expert_block, (hf.reshape(nblk, PEER_TOKEN_BLOCK, D_MODEL),
                                 idx.reshape(nblk, PEER_TOKEN_BLOCK, PEER_SLOTS),
                                 w.reshape(nblk, PEER_TOKEN_BLOCK, PEER_SLOTS)))
    return out.reshape(bsz, length, D_MODEL)


def trunk_layer(x, mod, lw, attend, ssd_h0):
    bsz, length = x.shape[:2]
    shift1, scale1, gate1, shift2, scale2, gate2 = [mod[..., i, :] for i in range(6)]
    h = rms_norm(x, lw['norm1_g']) * (1 + scale1) + shift1
    q, k, v, conv_in, z, xbc, dt_raw, gates = split_in(h @ lw['w_in'])
    q = q.reshape(bsz, length, N_Q_HEADS, HEAD_DIM)
    k = k.reshape(bsz, length, N_KV_HEADS, HEAD_DIM)
    v = v.reshape(bsz, length, N_KV_HEADS, HEAD_DIM)
    a_out = attend(q, k, v, lw['attn_sink']) @ lw['w_attn_o']
    c_out = conformer_conv(conv_in, lw['conv_dw_w'], lw['conv_dw_b'], lw['conv_ln_g'], lw['conv_ln_b']) @ lw['w_conv_o']
    s_y, s_final = ssd_bidirectional(xbc, dt_raw, z, lw['ssd_conv_w'], lw['ssd_conv_b'], lw['ssd_A_log'],
                                     lw['ssd_dt_bias'], lw['ssd_D'], lw['ssd_norm_g'], ssd_h0)
    s_out = s_y @ lw['w_ssd_o']
    g = jax.nn.sigmoid(gates.reshape(bsz, length, N_BRANCH, D_MODEL).astype(jnp.float32)
                       + lw['gate_b'].astype(jnp.float32)).astype(x.dtype)
    merged = g[..., 0, :] * a_out + g[..., 1, :] * c_out + g[..., 2, :] * s_out
    x = x + gate1 * (merged @ lw['w_out'])
    h2 = rms_norm(x, lw['norm2_g']) * (1 + scale2) + shift2
    x = x + gate2 * peer(h2, lw['peer_w_q'], lw['peer_sub_keys'], lw['peer_u'], lw['peer_v'])
    return x, k, v, s_final


def setup_inputs(seed: int = 0) -> dict:
    key = jax.random.key(seed)
    ks = iter(jax.random.split(key, 48))
    f32 = jnp.float32

    def nrm(shape, scale):
        return jax.random.normal(next(ks), shape, f32) * scale

    dt0 = jnp.exp(jax.random.uniform(next(ks), (DEPTH, 2, SSD_HEADS), f32, math.log(1e-3), math.log(1e-1)))
    return {
        'x_prompt': nrm((BATCH, SEQ, D_MODEL), 1.0),
        'x_sample': nrm((DEC_BATCH, DEC_SEQ, D_MODEL), 1.0),
        'cache_k': nrm((DEC_BATCH, DEPTH, PAST_LEN, N_KV_HEADS, HEAD_DIM), 1.0),
        'cache_v': nrm((DEC_BATCH, DEPTH, PAST_LEN, N_KV_HEADS, HEAD_DIM), 1.0),
        'state_ssd': nrm((DEC_BATCH, DEPTH, 2, SSD_HEADS, SSD_HEAD_DIM, D_STATE), 0.1),
        'c': nrm((DEC_BATCH, D_MODEL), 1.0),
        'c_ctx': nrm((D_MODEL,), 1.0),
        'w_ada': nrm((DEPTH, D_MODEL, 6 * D_MODEL), 0.5 * D_MODEL ** -0.5),
        'b_ada': nrm((DEPTH, 6 * D_MODEL), 0.02),
        'norm1_g': 1.0 + nrm((DEPTH, D_MODEL), 0.01),
        'norm2_g': 1.0 + nrm((DEPTH, D_MODEL), 0.01),
        'w_in': nrm((DEPTH, D_MODEL, IN_WIDTH), D_MODEL ** -0.5),
        'gate_b': nrm((DEPTH, N_BRANCH, D_MODEL), 0.01),
        'attn_sink': nrm((DEPTH, N_Q_HEADS), 0.5),
        'w_attn_o': nrm((DEPTH, ATTN_WIDTH, D_MODEL), ATTN_WIDTH ** -0.5),
        'conv_dw_w': nrm((DEPTH, CONV_KERNEL, CONV_WIDTH), CONV_KERNEL ** -0.5),
        'conv_dw_b': nrm((DEPTH, CONV_WIDTH), 0.01),
        'conv_ln_g': 1.0 + nrm((DEPTH, CONV_WIDTH), 0.01),
        'conv_ln_b': nrm((DEPTH, CONV_WIDTH), 0.01),
        'w_conv_o': nrm((DEPTH, CONV_WIDTH, D_MODEL), CONV_WIDTH ** -0.5),
        'ssd_conv_w': nrm((DEPTH, 2, SSD_CONV, XBC_WIDTH), SSD_CONV ** -0.5),
        'ssd_conv_b': nrm((DEPTH, 2, XBC_WIDTH), 0.01),
        'ssd_A_log': jnp.log(jax.random.uniform(next(ks), (DEPTH, 2, SSD_HEADS), f32, 1.0, 16.0)),
        'ssd_dt_bias': dt0 + jnp.log(-jnp.expm1(-dt0)),
        'ssd_D': 1.0 + nrm((DEPTH, 2, SSD_HEADS), 0.1),
        'ssd_norm_g': 1.0 + nrm((DEPTH, SSD_INNER), 0.01),
        'w_ssd_o': nrm((DEPTH, SSD_INNER, D_MODEL), SSD_INNER ** -0.5),
        'w_out': nrm((DEPTH, D_MODEL, D_MODEL), D_MODEL ** -0.5),
        'peer_w_q': nrm((DEPTH, D_MODEL, PEER_HEADS * PEER_KEY_DIM), D_MODEL ** -0.5),
        'peer_sub_keys': nrm((DEPTH, PEER_HEADS, 2, N_KEYS, PEER_HALF), PEER_HALF ** -0.5),
        'peer_u': nrm((DEPTH, N_EXPERTS, D_MODEL), D_MODEL ** -0.5),
        'peer_v': nrm((DEPTH, N_EXPERTS, D_MODEL), PEER_SLOTS ** -0.5),
        'final_g': 1.0 + nrm((D_MODEL,), 0.01),
    }


def reference(x_prompt, x_sample, cache_k, cache_v, state_ssd, c, c_ctx, w_ada, b_ada, norm1_g, norm2_g,
              w_in, gate_b, attn_sink, w_attn_o, conv_dw_w, conv_dw_b, conv_ln_g, conv_ln_b, w_conv_o,
              ssd_conv_w, ssd_conv_b, ssd_A_log, ssd_dt_bias, ssd_D, ssd_norm_g, w_ssd_o, w_out,
              peer_w_q, peer_sub_keys, peer_u, peer_v, final_g):
    ang_row, ang_col = grid_rope_angles(x_sample.shape[1])
    h0_ctx = jnp.zeros((x_prompt.shape[0], 2, SSD_HEADS, SSD_HEAD_DIM, D_STATE), jnp.float32)
    xp = x_prompt
    xs = x_sample
    new_k = []
    new_v = []
    new_s = []
    for l in range(DEPTH):
        lw = dict(norm1_g=norm1_g[l], norm2_g=norm2_g[l], w_in=w_in[l], gate_b=gate_b[l],
                  attn_sink=attn_sink[l], w_attn_o=w_attn_o[l], conv_dw_w=conv_dw_w[l],
                  conv_dw_b=conv_dw_b[l], conv_ln_g=conv_ln_g[l], conv_ln_b=conv_ln_b[l],
                  w_conv_o=w_conv_o[l], ssd_conv_w=ssd_conv_w[l], ssd_conv_b=ssd_conv_b[l],
                  ssd_A_log=ssd_A_log[l], ssd_dt_bias=ssd_dt_bias[l], ssd_D=ssd_D[l],
                  ssd_norm_g=ssd_norm_g[l], w_ssd_o=w_ssd_o[l], w_out=w_out[l],
                  peer_w_q=peer_w_q[l], peer_sub_keys=peer_sub_keys[l], peer_u=peer_u[l], peer_v=peer_v[l])
        mod_ctx = modulation(c_ctx, w_ada[l], b_ada[l])[None, None]
        xp, k_l, v_l, s_l = trunk_layer(xp, mod_ctx, lw, ctx_attention, h0_ctx)
        new_k.append(k_l)
        new_v.append(v_l)
        new_s.append(s_l)
        mod_lat = modulation(c, w_ada[l], b_ada[l])[:, None]
        attend = functools.partial(latent_attention, ck=cache_k[:, l], cv=cache_v[:, l],
                                   ang_row=ang_row, ang_col=ang_col)
        xs, _, _, _ = trunk_layer(xs, mod_lat, lw, attend, state_ssd[:, l])
    y_prompt = rms_norm(xp, final_g)
    y_sample = rms_norm(xs, final_g)
    new_cache_k = jnp.stack(new_k, axis=1)
    new_cache_v = jnp.stack(new_v, axis=1)
    new_state_ssd = jnp.stack(new_s, axis=1)
    return (y_prompt, y_sample, new_cache_k, new_cache_v, new_state_ssd)
```

```python
import functools
import math

import jax
import jax.numpy as jnp
from jax import lax
from jax.experimental import pallas as pl
from jax.experimental.pallas import tpu as pltpu

F32 = jnp.float32
BF16 = jnp.bfloat16
I32 = jnp.int32

D_MODEL = 2048
GRID_W = 64
BLOCK = 128
EPS = 1e-6
NEG_INF = -1e30
N_Q_HEADS = 16
N_KV_HEADS = 4
HEAD_DIM = 64
Q_PER_KV = N_Q_HEADS // N_KV_HEADS
ROPE_BASE = 10000.0
ROPE_FREQS = HEAD_DIM // 4
ATTN_SCALE = HEAD_DIM ** -0.5
ATTN_WIDTH = N_Q_HEADS * HEAD_DIM
KV_WIDTH = N_KV_HEADS * HEAD_DIM
CONV_WIDTH = 1024
CONV_KERNEL = 31
CONV_PAD = CONV_KERNEL // 2
CONV_HALO = 16
SSD_HEADS = 16
SSD_HEAD_DIM = 64
SSD_INNER = SSD_HEADS * SSD_HEAD_DIM
SSD_GROUPS = 2
SSD_HPG = SSD_HEADS // SSD_GROUPS
D_STATE = 128
SSD_CONV = 4
SSD_CHUNK = 128
XBC_WIDTH = SSD_INNER + 2 * SSD_GROUPS * D_STATE
PEER_HEADS = 8
N_KEYS = 128
N_EXPERTS = N_KEYS * N_KEYS
PEER_HALF = 128
PEER_TOPK = 16
PEER_SLOTS = PEER_HEADS * PEER_TOPK
N_BRANCH = 3
LANES = 128

COL_XBC = 0
COL_K = XBC_WIDTH
COL_V = COL_K + KV_WIDTH
COL_CONV = COL_V + KV_WIDTH
COL_GATES = COL_CONV + 2 * CONV_WIDTH
COL_Q = COL_GATES + N_BRANCH * D_MODEL
COL_Z = COL_Q + ATTN_WIDTH
U_WIDTH = COL_Z + SSD_INNER

VMEM_LIMIT = 48 * 1024 * 1024


def _cparams(*sem):
    return pltpu.CompilerParams(dimension_semantics=sem, vmem_limit_bytes=VMEM_LIMIT)


def _mm_kernel(*refs, silu_in, has_bias, has_res):
    x_ref, w_ref = refs[0], refs[1]
    pos = 2
    x = x_ref[...]
    if silu_in:
        x = x * jax.nn.sigmoid(x)
    acc = jnp.dot(x.astype(BF16), w_ref[...].astype(BF16), preferred_element_type=F32)
    if has_bias:
        acc = acc + refs[pos][...]
        pos += 1
    if has_res:
        acc = refs[pos][...] + refs[pos + 1][...] * acc
        pos += 2
    o_ref = refs[pos]
    o_ref[...] = acc.astype(o_ref.dtype)


def _matmul(x, w, *, tm, tn, out_dtype=F32, silu_in=False, bias=None, res=None, gate=None, gate_row=None):
    m, k = x.shape
    n = w.shape[1]
    in_specs = [pl.BlockSpec((tm, k), lambda i, j: (i, 0)), pl.BlockSpec((k, tn), lambda i, j: (0, j))]
    args = [x, w]
    if bias is not None:
        in_specs.append(pl.BlockSpec((1, tn), lambda i, j: (0, j)))
        args.append(bias)
    if res is not None:
        in_specs.append(pl.BlockSpec((tm, tn), lambda i, j: (i, j)))
        in_specs.append(pl.BlockSpec((None, 1, tn), lambda i, j: (gate_row(i), 0, j)))
        args += [res, gate]
    return pl.pallas_call(
        functools.partial(_mm_kernel, silu_in=silu_in, has_bias=bias is not None, has_res=res is not None),
        grid=(m // tm, n // tn),
        in_specs=in_specs,
        out_specs=pl.BlockSpec((tm, tn), lambda i, j: (i, j)),
        out_shape=jax.ShapeDtypeStruct((m, n), out_dtype),
        compiler_params=_cparams("parallel", "arbitrary"),
    )(*args)


def _norm_mod_kernel(x_ref, g_ref, shift_ref, scale_ref, *o_refs):
    x = x_ref[...]
    y = x * lax.rsqrt(jnp.mean(x * x, axis=-1, keepdims=True) + EPS)
    h = (y * g_ref[...]) * (1.0 + scale_ref[...]) + shift_ref[...]
    for o_ref in o_refs:
        o_ref[...] = h.astype(o_ref.dtype)


def _norm_mod(x, g, shift, scale, row_of_tile, *, tm, out_dtypes):
    m, d = x.shape
    mod_spec = pl.BlockSpec((None, 1, d), lambda i: (row_of_tile(i), 0, 0))
    outs = pl.pallas_call(
        _norm_mod_kernel,
        grid=(m // tm,),
        in_specs=[pl.BlockSpec((tm, d), lambda i: (i, 0)), pl.BlockSpec((1, d), lambda i: (0, 0)), mod_spec, mod_spec],
        out_specs=[pl.BlockSpec((tm, d), lambda i: (i, 0)) for _ in out_dtypes],
        out_shape=[jax.ShapeDtypeStruct((m, d), dt) for dt in out_dtypes],
        compiler_params=_cparams("parallel"),
    )(x, g, shift, scale)
    return outs


def _final_norm_kernel(x_ref, g_ref, o_ref):
    x = x_ref[...]
    o_ref[...] = x * lax.rsqrt(jnp.mean(x * x, axis=-1, keepdims=True) + EPS) * g_ref[...]


def _final_norm(x, g, *, tm):
    m, d = x.shape
    return pl.pallas_call(
        _final_norm_kernel,
        grid=(m // tm,),
        in_specs=[pl.BlockSpec((tm, d), lambda i: (i, 0)), pl.BlockSpec((1, d), lambda i: (0, 0))],
        out_specs=pl.BlockSpec((tm, d), lambda i: (i, 0)),
        out_shape=jax.ShapeDtypeStruct((m, d), F32),
        compiler_params=_cparams("parallel"),
    )(x, g)


def _sink_column(sink_ref, kh, rows):
    grp = lax.broadcasted_iota(I32, (rows, 1), 0) // (rows // Q_PER_KV)
    col = jnp.full((rows, 1), sink_ref[kh * Q_PER_KV], F32)
    for g in range(1, Q_PER_KV):
        col = jnp.where(grp == g, sink_ref[kh * Q_PER_KV + g], col)
    return col


def _ctx_attn_kernel(sink_ref, q_ref, k_ref, v_ref, o_ref):
    length = q_ref.shape[0]
    for kh in range(N_KV_HEADS):
        ksl = slice(kh * HEAD_DIM, (kh + 1) * HEAD_DIM)
        kk = k_ref[:, ksl].astype(BF16)
        vv = v_ref[:, ksl].astype(BF16)
        qq = jnp.concatenate(
            [q_ref[:, (kh * Q_PER_KV + g) * HEAD_DIM:(kh * Q_PER_KV + g + 1) * HEAD_DIM] for g in range(Q_PER_KV)],
            axis=0).astype(BF16)
        s = lax.dot_general(qq, kk, (((1,), (1,)), ((), ())), preferred_element_type=F32) * ATTN_SCALE
        sink = _sink_column(sink_ref, kh, Q_PER_KV * length)
        m = jnp.maximum(jnp.max(s, axis=-1, keepdims=True), sink)
        p = jnp.exp(s - m)
        denom = jnp.sum(p, axis=-1, keepdims=True) + jnp.exp(sink - m)
        o = jnp.dot((p / denom).astype(BF16), vv, preferred_element_type=F32)
        for g in range(Q_PER_KV):
            h = kh * Q_PER_KV + g
            o_ref[:, h * HEAD_DIM:(h + 1) * HEAD_DIM] = o[g * length:(g + 1) * length].astype(o_ref.dtype)


def _ctx_attention(u, sink, *, row0, nseq, length):
    r0 = row0 // length
    return pl.pallas_call(
        _ctx_attn_kernel,
        grid=(nseq,),
        in_specs=[
            pl.BlockSpec(memory_space=pltpu.SMEM),
            pl.BlockSpec((length, ATTN_WIDTH), lambda b: (r0 + b, COL_Q // ATTN_WIDTH)),
            pl.BlockSpec((length, KV_WIDTH), lambda b: (r0 + b, COL_K // KV_WIDTH)),
            pl.BlockSpec((length, KV_WIDTH), lambda b: (r0 + b, COL_V // KV_WIDTH)),
        ],
        out_specs=pl.BlockSpec((length, ATTN_WIDTH), lambda b: (b, 0)),
        out_shape=jax.ShapeDtypeStruct((nseq * length, ATTN_WIDTH), BF16),
        compiler_params=_cparams("parallel"),
    )(sink, u, u, u)


def _rope_kernel(q_ref, k_ref, cos_ref, sin_ref, qo_ref, ko_ref):
    def rot(x, cos, sin):
        width = x.shape[1]
        lane = lax.broadcasted_iota(I32, x.shape, 1)
        first = (lane // ROPE_FREQS) % 2 == 0
        partner = jnp.where(first, pltpu.roll(x, width - ROPE_FREQS, axis=1), pltpu.roll(x, ROPE_FREQS, axis=1))
        return x * cos + partner * sin

    qo_ref[...] = (rot(q_ref[...], cos_ref[...], sin_ref[...]) * ATTN_SCALE).astype(qo_ref.dtype)
    ko_ref[...] = rot(k_ref[...], cos_ref[:, :KV_WIDTH], sin_ref[:, :KV_WIDTH]).astype(ko_ref.dtype)


def _rope(u, cos, sin, *, row0, nrows, length, tm):
    r0 = row0 // tm
    per_seq = length // tm
    return pl.pallas_call(
        _rope_kernel,
        grid=(nrows // tm,),
        in_specs=[
            pl.BlockSpec((tm, ATTN_WIDTH), lambda i: (r0 + i, COL_Q // ATTN_WIDTH)),
            pl.BlockSpec((tm, KV_WIDTH), lambda i: (r0 + i, COL_K // KV_WIDTH)),
            pl.BlockSpec((tm, ATTN_WIDTH), lambda i: (i % per_seq, 0)),
            pl.BlockSpec((tm, ATTN_WIDTH), lambda i: (i % per_seq, 0)),
        ],
        out_specs=[pl.BlockSpec((tm, ATTN_WIDTH), lambda i: (i, 0)), pl.BlockSpec((tm, KV_WIDTH), lambda i: (i, 0))],
        out_shape=[jax.ShapeDtypeStruct((nrows, ATTN_WIDTH), BF16), jax.ShapeDtypeStruct((nrows, KV_WIDTH), BF16)],
        compiler_params=_cparams("parallel"),
    )(u, u, cos, sin)


def _lat_attn_kernel(sink_ref, q_ref, kp_ref, kc_ref, kn_ref, vp_ref, vc_ref, vn_ref, ck_ref, cv_ref, o_ref):
    n = pl.program_id(1)
    nb = pl.num_programs(1)
    rows = Q_PER_KV * BLOCK
    r = lax.broadcasted_iota(I32, (rows, 3 * BLOCK), 0) % BLOCK
    c = lax.broadcasted_iota(I32, (rows, 3 * BLOCK), 1)
    lo = jnp.where(n > 0, 0, BLOCK)
    hi = jnp.where(n < nb - 1, 3 * BLOCK, 2 * BLOCK)
    mask = (c >= r) & (c <= r + 2 * BLOCK) & (c >= lo) & (c < hi)
    for kh in range(N_KV_HEADS):
        ksl = slice(kh * HEAD_DIM, (kh + 1) * HEAD_DIM)
        kk = jnp.concatenate([kp_ref[:, ksl], kc_ref[:, ksl], kn_ref[:, ksl], ck_ref[:, ksl].astype(BF16)], axis=0)
        vv = jnp.concatenate([vp_ref[:, ksl], vc_ref[:, ksl], vn_ref[:, ksl], cv_ref[:, ksl]], axis=0).astype(BF16)
        qq = jnp.concatenate(
            [q_ref[:, (kh * Q_PER_KV + g) * HEAD_DIM:(kh * Q_PER_KV + g + 1) * HEAD_DIM] for g in range(Q_PER_KV)],
            axis=0)
        s = lax.dot_general(qq, kk, (((1,), (1,)), ((), ())), preferred_element_type=F32)
        s_win = jnp.where(mask, s[:, :3 * BLOCK], NEG_INF)
        s_ctx = s[:, 3 * BLOCK:]
        sink = _sink_column(sink_ref, kh, rows)
        m = jnp.maximum(jnp.maximum(jnp.max(s_win, axis=-1, keepdims=True), jnp.max(s_ctx, axis=-1, keepdims=True)),
                        sink)
        p_win = jnp.exp(s_win - m)
        p_ctx = jnp.exp(s_ctx - m)
        denom = (jnp.sum(p_win, axis=-1, keepdims=True) + jnp.sum(p_ctx, axis=-1, keepdims=True)
                 + jnp.exp(sink - m))
        inv = 1.0 / denom
        p = jnp.concatenate([p_win * inv, p_ctx * inv], axis=1).astype(BF16)
        o = jnp.dot(p, vv, preferred_element_type=F32)
        for g in range(Q_PER_KV):
            h = kh * Q_PER_KV + g
            o_ref[:, h * HEAD_DIM:(h + 1) * HEAD_DIM] = o[g * BLOCK:(g + 1) * BLOCK].astype(o_ref.dtype)


def _lat_attention(q_rot, k_rot, u, ck, cv, sink, *, row0, nseq, length):
    nb = length // BLOCK
    r0 = row0 // BLOCK

    def prev(b, n):
        return jnp.maximum(n - 1, 0)

    def nxt(b, n):
        return jnp.minimum(n + 1, nb - 1)

    vcol = COL_V // KV_WIDTH
    past = ck.shape[1]
    return pl.pallas_call(
        _lat_attn_kernel,
        grid=(nseq, nb),
        in_specs=[
            pl.BlockSpec(memory_space=pltpu.SMEM),
            pl.BlockSpec((BLOCK, ATTN_WIDTH), lambda b, n: (b * nb + n, 0)),
            pl.BlockSpec((BLOCK, KV_WIDTH), lambda b, n: (b * nb + prev(b, n), 0)),
            pl.BlockSpec((BLOCK, KV_WIDTH), lambda b, n: (b * nb + n, 0)),
            pl.BlockSpec((BLOCK, KV_WIDTH), lambda b, n: (b * nb + nxt(b, n), 0)),
            pl.BlockSpec((BLOCK, KV_WIDTH), lambda b, n: (r0 + b * nb + prev(b, n), vcol)),
            pl.BlockSpec((BLOCK, KV_WIDTH), lambda b, n: (r0 + b * nb + n, vcol)),
            pl.BlockSpec((BLOCK, KV_WIDTH), lambda b, n: (r0 + b * nb + nxt(b, n), vcol)),
            pl.BlockSpec((None, past, KV_WIDTH), lambda b, n: (b, 0, 0)),
            pl.BlockSpec((None, past, KV_WIDTH), lambda b, n: (b, 0, 0)),
        ],
        out_specs=pl.BlockSpec((BLOCK, ATTN_WIDTH), lambda b, n: (b * nb + n, 0)),
        out_shape=jax.ShapeDtypeStruct((nseq * length, ATTN_WIDTH), BF16),
        compiler_params=_cparams("parallel", "arbitrary"),
    )(sink, q_rot, k_rot, k_rot, k_rot, u, u, u, ck, cv)


def _conv_kernel(prev_ref, cur_ref, next_ref, w_ref, b_ref, lng_ref, lnb_ref, o_ref, hp_ref, *, ctx_blocks, per_seq):
    i = pl.program_id(0)
    tb = cur_ref.shape[0]
    j = (i - ctx_blocks) % per_seq
    is_lat = i >= ctx_blocks
    has_prev = jnp.logical_and(is_lat, j != 0)
    has_next = jnp.logical_and(is_lat, j != per_seq - 1)

    def glu(x):
        return x[:, :CONV_WIDTH] * jax.nn.sigmoid(x[:, CONV_WIDTH:])

    hp_ref[0:CONV_HALO, :] = jnp.where(has_prev, glu(prev_ref[...]), 0.0)
    hp_ref[CONV_HALO:CONV_HALO + tb, :] = glu(cur_ref[...])
    hp_ref[CONV_HALO + tb:2 * CONV_HALO + tb, :] = jnp.where(has_next, glu(next_ref[...]), 0.0)
    acc = jnp.zeros((tb, CONV_WIDTH), F32)
    off = CONV_HALO - CONV_PAD
    for t in range(CONV_KERNEL):
        acc = acc + w_ref[t:t + 1, :] * hp_ref[off + t:off + t + tb, :]
    acc = acc + b_ref[...]
    mu = jnp.mean(acc, axis=-1, keepdims=True)
    xc = acc - mu
    var = jnp.mean(xc * xc, axis=-1, keepdims=True)
    y = xc * lax.rsqrt(var + EPS) * lng_ref[...] + lnb_ref[...]
    o_ref[...] = (y * jax.nn.sigmoid(y)).astype(o_ref.dtype)


def _conformer(u, dw_w, dw_b, ln_g, ln_b, *, ctx_rows, ctx_len, lat_len, tb):
    assert ctx_len == tb
    m = u.shape[0]
    nblk = m // tb
    ctx_blocks = ctx_rows // tb
    per_seq = lat_len // tb
    hb = tb // CONV_HALO
    ccol = COL_CONV // (2 * CONV_WIDTH)
    last = m // CONV_HALO - 1
    return pl.pallas_call(
        functools.partial(_conv_kernel, ctx_blocks=ctx_blocks, per_seq=per_seq),
        grid=(nblk,),
        in_specs=[
            pl.BlockSpec((CONV_HALO, 2 * CONV_WIDTH), lambda i: (jnp.maximum(i * hb - 1, 0), ccol)),
            pl.BlockSpec((tb, 2 * CONV_WIDTH), lambda i: (i, ccol)),
            pl.BlockSpec((CONV_HALO, 2 * CONV_WIDTH), lambda i: (jnp.minimum((i + 1) * hb, last), ccol)),
            pl.BlockSpec((CONV_KERNEL, CONV_WIDTH), lambda i: (0, 0)),
            pl.BlockSpec((1, CONV_WIDTH), lambda i: (0, 0)),
            pl.BlockSpec((1, CONV_WIDTH), lambda i: (0, 0)),
            pl.BlockSpec((1, CONV_WIDTH), lambda i: (0, 0)),
        ],
        out_specs=pl.BlockSpec((tb, CONV_WIDTH), lambda i: (i, 0)),
        out_shape=jax.ShapeDtypeStruct((m, CONV_WIDTH), BF16),
        scratch_shapes=[pltpu.VMEM((tb + 2 * CONV_HALO, CONV_WIDTH), F32)],
        compiler_params=_cparams("parallel"),
    )(u, u, u, dw_w, dw_b, ln_g, ln_b)


def _exact_dot(a, b):
    return jnp.dot(a, b, precision=lax.Precision.HIGHEST, preferred_element_type=F32)


def _ssd_kernel(xbc_ref, dt_ref, cw_ref, cb_ref, dtb_ref, a_ref, e_ref, dskip_ref, h0_ref, y_ref, st_ref,
                ext_ref, *, reverse, lane0):
    c = pl.program_id(1)
    L = SSD_CHUNK
    pad = 8
    cur = xbc_ref[...]

    @pl.when(c == 0)
    def _():
        st_ref[...] = h0_ref[...]
        ext_ref[...] = jnp.zeros_like(ext_ref)

    ext_ref[pad:pad + L, :] = cur
    conv = jnp.zeros((L, XBC_WIDTH), F32)
    for j in range(SSD_CONV):
        off = pad + (SSD_CONV - 1 - j) if reverse else pad - (SSD_CONV - 1) + j
        conv = conv + cw_ref[j:j + 1, :] * ext_ref[off:off + L, :]
    conv = conv + cb_ref[...]
    if reverse:
        ext_ref[pad + L:pad + L + pad, :] = cur[0:pad, :]
    else:
        ext_ref[0:pad, :] = cur[L - pad:L, :]
    act = conv * jax.nn.sigmoid(conv)
    xs = act[:, :SSD_INNER]

    dt = jax.nn.softplus(dt_ref[...] + dtb_ref[...])
    da = dt * a_ref[...]
    ti = lax.broadcasted_iota(I32, (L, L), 0)
    si = lax.broadcasted_iota(I32, (L, L), 1)
    tri = (si >= ti) if reverse else (si <= ti)
    acum = _exact_dot(tri.astype(F32), da)
    acum_t = acum.T
    total = acum[0:1, :] if reverse else acum[L - 1:L, :]
    to_end = jnp.exp(total - acum)
    fac = jnp.concatenate([dt, dt * to_end, jnp.exp(acum), jnp.broadcast_to(jnp.exp(total), (pad, LANES))], axis=0)
    fac = _exact_dot(fac, e_ref[...])
    xg = (xs * fac[0:L]).astype(BF16)
    xg_end = (xs * fac[L:2 * L]).astype(BF16)
    eac = fac[2 * L:3 * L]
    dec = fac[3 * L:3 * L + 1]

    y_ref[...] = dskip_ref[...] * xs
    gw = SSD_HPG * SSD_HEAD_DIM
    for g in range(SSD_GROUPS):
        bg = act[:, SSD_INNER + g * D_STATE:SSD_INNER + (g + 1) * D_STATE]
        cg = act[:, SSD_INNER + SSD_GROUPS * D_STATE + g * D_STATE:SSD_INNER + SSD_GROUPS * D_STATE + (g + 1) * D_STATE]
        bgb = bg.astype(BF16)
        cgb = cg.astype(BF16)
        cbm = lax.dot_general(cgb, bgb, (((1,), (1,)), ((), ())), preferred_element_type=F32)
        st = st_ref[g]
        y_off = jnp.dot(cgb, st.astype(BF16), preferred_element_type=F32) * eac[:, g * gw:(g + 1) * gw]
        y_ref[:, g * gw:(g + 1) * gw] += y_off
        st_ref[g] = st * dec[:, g * gw:(g + 1) * gw] + jnp.dot(
            bg.T.astype(BF16), xg_end[:, g * gw:(g + 1) * gw], preferred_element_type=F32)
        for hh in range(SSD_HPG):
            h = g * SSD_HPG + hh
            lane = lane0 + h
            seg = acum[:, lane:lane + 1] - acum_t[lane:lane + 1, :]
            decay = jnp.exp(jnp.where(tri, seg, -jnp.inf))
            mat = (cbm * decay).astype(BF16)
            y_ref[:, h * SSD_HEAD_DIM:(h + 1) * SSD_HEAD_DIM] += jnp.dot(
                mat, xg[:, h * SSD_HEAD_DIM:(h + 1) * SSD_HEAD_DIM], preferred_element_type=F32)


def _ssd_direction(u, udt, conv_w, conv_b, dt_bias, a_neg, expand, dskip, h0, *, reverse, row0, nseq, length):
    nc = length // SSD_CHUNK
    r0 = row0 // SSD_CHUNK

    def chunk(s, c):
        return r0 + s * nc + (nc - 1 - c if reverse else c)

    return pl.pallas_call(
        functools.partial(_ssd_kernel, reverse=reverse, lane0=SSD_HEADS if reverse else 0),
        grid=(nseq, nc),
        in_specs=[
            pl.BlockSpec((SSD_CHUNK, XBC_WIDTH), lambda s, c: (chunk(s, c), COL_XBC // XBC_WIDTH)),
            pl.BlockSpec((SSD_CHUNK, LANES), lambda s, c: (chunk(s, c), 0)),
            pl.BlockSpec((SSD_CONV, XBC_WIDTH), lambda s, c: (0, 0)),
            pl.BlockSpec((1, XBC_WIDTH), lambda s, c: (0, 0)),
            pl.BlockSpec((1, LANES), lambda s, c: (0, 0)),
            pl.BlockSpec((1, LANES), lambda s, c: (0, 0)),
            pl.BlockSpec((LANES, SSD_INNER), lambda s, c: (0, 0)),
            pl.BlockSpec((1, SSD_INNER), lambda s, c: (0, 0)),
            pl.BlockSpec((None, SSD_GROUPS, D_STATE, SSD_HPG * SSD_HEAD_DIM), lambda s, c: (s, 0, 0, 0)),
        ],
        out_specs=[
            pl.BlockSpec((SSD_CHUNK, SSD_INNER), lambda s, c: (chunk(s, c) - r0, 0)),
            pl.BlockSpec((None, SSD_GROUPS, D_STATE, SSD_HPG * SSD_HEAD_DIM), lambda s, c: (s, 0, 0, 0)),
        ],
        out_shape=[
            jax.ShapeDtypeStruct((nseq * length, SSD_INNER), F32),
            jax.ShapeDtypeStruct((nseq, SSD_GROUPS, D_STATE, SSD_HPG * SSD_HEAD_DIM), F32),
        ],
        scratch_shapes=[pltpu.VMEM((SSD_CHUNK + 16, XBC_WIDTH), F32)],
        compiler_params=_cparams("parallel", "arbitrary"),
    )(u, udt, conv_w, conv_b, dt_bias, a_neg, expand, dskip, h0)


def _ssd_gate_kernel(yf_ref, yb_ref, z_ref, g_ref, o_ref):
    z = z_ref[...]
    y = (yf_ref[...] + yb_ref[...]) * (z * jax.nn.sigmoid(z))
    o_ref[...] = (y * lax.rsqrt(jnp.mean(y * y, axis=-1, keepdims=True) + EPS) * g_ref[...]).astype(o_ref.dtype)


def _ssd_gate(yf, yb, u, norm_g, *, tm):
    m = yf.shape[0]
    return pl.pallas_call(
        _ssd_gate_kernel,
        grid=(m // tm,),
        in_specs=[
            pl.BlockSpec((tm, SSD_INNER), lambda i: (i, 0)),
            pl.BlockSpec((tm, SSD_INNER), lambda i: (i, 0)),
            pl.BlockSpec((tm, SSD_INNER), lambda i: (i, COL_Z // SSD_INNER)),
            pl.BlockSpec((1, SSD_INNER), lambda i: (0, 0)),
        ],
        out_specs=pl.BlockSpec((tm, SSD_INNER), lambda i: (i, 0)),
        out_shape=jax.ShapeDtypeStruct((m, SSD_INNER), BF16),
        compiler_params=_cparams("parallel"),
    )(yf, yb, u, norm_g)


def _merge_kernel(a_ref, c_ref, s_ref, wa_ref, wc_ref, ws_ref, g0_ref, g1_ref, g2_ref, gb_ref, o_ref):
    acc = None
    for b, (x_ref, w_ref, g_ref) in enumerate(((a_ref, wa_ref, g0_ref), (c_ref, wc_ref, g1_ref), (s_ref, ws_ref, g2_ref))):
        gate = jax.nn.sigmoid(g_ref[...] + gb_ref[b:b + 1, :])
        term = gate * jnp.dot(x_ref[...], w_ref[...], preferred_element_type=F32)
        acc = term if acc is None else acc + term
    o_ref[...] = acc.astype(o_ref.dtype)


def _merge(attn, conv, ssd, w_a, w_c, w_s, u, gate_b, *, tm, tn):
    m = attn.shape[0]
    width = attn.shape[1]
    gcol = COL_GATES // tn
    per = D_MODEL // tn
    act_spec = pl.BlockSpec((tm, width), lambda i, j: (i, 0))
    w_spec = pl.BlockSpec((width, tn), lambda i, j: (0, j))
    return pl.pallas_call(
        _merge_kernel,
        grid=(m // tm, D_MODEL // tn),
        in_specs=[act_spec, act_spec, act_spec, w_spec, w_spec, w_spec,
                  pl.BlockSpec((tm, tn), lambda i, j: (i, gcol + j)),
                  pl.BlockSpec((tm, tn), lambda i, j: (i, gcol + per + j)),
                  pl.BlockSpec((tm, tn), lambda i, j: (i, gcol + 2 * per + j)),
                  pl.BlockSpec((N_BRANCH, tn), lambda i, j: (0, j))],
        out_specs=pl.BlockSpec((tm, tn), lambda i, j: (i, j)),
        out_shape=jax.ShapeDtypeStruct((m, D_MODEL), BF16),
        compiler_params=_cparams("parallel", "arbitrary"),
    )(attn, conv, ssd, w_a, w_c, w_s, u, u, u, gate_b)


def _top16(s, payload=None):
    rows = s.shape[0]
    iota = lax.broadcasted_iota(I32, s.shape, 0).astype(F32)
    vals, poss, pays = [], [], []
    for _ in range(PEER_TOPK):
        m = jnp.max(s, axis=0, keepdims=True)
        p = jnp.min(jnp.where(s == m, iota, float(rows)), axis=0, keepdims=True)
        hit = iota == p
        vals.append(m)
        poss.append(p)
        if payload is not None:
            pays.append(jnp.sum(jnp.where(hit, payload, 0.0), axis=0, keepdims=True))
        s = jnp.where(hit, -jnp.inf, s)
    vals = jnp.concatenate(vals, axis=0)
    if payload is not None:
        return vals, jnp.concatenate(pays, axis=0)
    return vals, jnp.concatenate(poss, axis=0)


def _peer_topk_kernel(q_ref, keys_ref, idx_ref, w_ref):
    for h in range(PEER_HEADS):
        halves = []
        for half in range(2):
            j = h * 2 + half
            qh = q_ref[:, j * PEER_HALF:(j + 1) * PEER_HALF].astype(BF16)
            st = lax.dot_general(keys_ref[j], qh, (((1,), (1,)), ((), ())), preferred_element_type=F32)
            halves.append(_top16(st))
        (v1, p1), (v2, p2) = halves
        cand = jnp.concatenate([v1[a:a + 1, :] + v2 for a in range(PEER_TOPK)], axis=0)
        cidx = jnp.concatenate([p1[a:a + 1, :] * N_KEYS + p2 for a in range(PEER_TOPK)], axis=0)
        top, idx = _top16(cand, cidx)
        e = jnp.exp(top - top[0:1, :])
        w = e / jnp.sum(e, axis=0, keepdims=True)
        idx_ref[h * PEER_TOPK:(h + 1) * PEER_TOPK, :] = idx.astype(I32)
        w_ref[h * PEER_TOPK:(h + 1) * PEER_TOPK, :] = w


def _peer_topk(q, keys, *, tm):
    m = q.shape[0]
    return pl.pallas_call(
        _peer_topk_kernel,
        grid=(m // tm,),
        in_specs=[pl.BlockSpec((tm, q.shape[1]), lambda i: (i, 0)),
                  pl.BlockSpec(keys.shape, lambda i: (0, 0, 0))],
        out_specs=[pl.BlockSpec((PEER_SLOTS, tm), lambda i: (0, i)), pl.BlockSpec((PEER_SLOTS, tm), lambda i: (0, i))],
        out_shape=[jax.ShapeDtypeStruct((PEER_SLOTS, m), I32), jax.ShapeDtypeStruct((PEER_SLOTS, m), F32)],
        compiler_params=_cparams("parallel"),
    )(q, keys)


PEER_TOKENS = 128
PEER_BUFS = 4


def _peer_expert_kernel(idx_ref, wt_ref, h_ref, x_ref, gate_ref, uv_ref, o_ref, gbuf, sems):
    lane = lax.broadcasted_iota(I32, (PEER_SLOTS, PEER_TOKENS), 1)

    def rows_copy(slot):
        return pltpu.make_async_copy(uv_ref.at[pl.ds(0, PEER_SLOTS)], gbuf.at[slot], sems.at[slot])

    def start_token(t, slot):
        for j in range(PEER_SLOTS):
            pltpu.make_async_copy(uv_ref.at[pl.ds(idx_ref[t, j], 1)], gbuf.at[slot, pl.ds(j, 1)], sems.at[slot]).start()

    for t in range(PEER_BUFS - 1):
        start_token(t, t)

    def body(t, carry):
        slot = t % PEER_BUFS
        nxt = t + PEER_BUFS - 1

        @pl.when(nxt < PEER_TOKENS)
        def _():
            start_token(nxt, nxt % PEER_BUFS)

        rows_copy(slot).wait()
        xb = h_ref[pl.ds(t, 1), :]
        a = jnp.sum(gbuf[slot, :, 0:D_MODEL] * xb, axis=1, keepdims=True)
        w = jnp.sum(jnp.where(lane == t, wt_ref[...], 0.0), axis=1, keepdims=True)
        coef = w * (0.5 * a * (1.0 + lax.erf(a * (1.0 / math.sqrt(2.0)))))
        out = jnp.sum(coef * gbuf[slot, :, D_MODEL:2 * D_MODEL], axis=0, keepdims=True)
        o_ref[pl.ds(t, 1), :] = x_ref[pl.ds(t, 1), :] + gate_ref[...] * out
        return carry

    lax.fori_loop(0, PEER_TOKENS, body, 0)


def _peer_experts(idx, wt, h, x, gate, gate_row, uv):
    m = h.shape[0]
    return pl.pallas_call(
        _peer_expert_kernel,
        grid=(m // PEER_TOKENS,),
        in_specs=[
            pl.BlockSpec((PEER_TOKENS, PEER_SLOTS), lambda i: (i, 0), memory_space=pltpu.SMEM),
            pl.BlockSpec((PEER_SLOTS, PEER_TOKENS), lambda i: (0, i)),
            pl.BlockSpec((PEER_TOKENS, D_MODEL), lambda i: (i, 0)),
            pl.BlockSpec((PEER_TOKENS, D_MODEL), lambda i: (i, 0)),
            pl.BlockSpec((None, 1, D_MODEL), lambda i: (gate_row(i), 0, 0)),
            pl.BlockSpec(memory_space=pl.ANY),
        ],
        out_specs=pl.BlockSpec((PEER_TOKENS, D_MODEL), lambda i: (i, 0)),
        out_shape=jax.ShapeDtypeStruct((m, D_MODEL), F32),
        scratch_shapes=[pltpu.VMEM((PEER_BUFS, PEER_SLOTS, 2 * D_MODEL), F32), pltpu.SemaphoreType.DMA((PEER_BUFS,))],
        compiler_params=_cparams("arbitrary"),
    )(idx, wt, h, x, gate, uv)


def _rope_tables(length):
    rows = length // GRID_W
    row = jnp.repeat(jnp.arange(rows, dtype=F32), GRID_W)
    col = jnp.tile(jnp.arange(GRID_W, dtype=F32), rows)
    freqs = ROPE_BASE ** (-jnp.arange(ROPE_FREQS, dtype=F32) / ROPE_FREQS)
    ar, ac = row[:, None] * freqs, col[:, None] * freqs
    cos = jnp.concatenate([jnp.cos(ar), jnp.cos(ar), jnp.cos(ac), jnp.cos(ac)], axis=-1)
    sin = jnp.concatenate([-jnp.sin(ar), jnp.sin(ar), -jnp.sin(ac), jnp.sin(ac)], axis=-1)
    return jnp.tile(cos, (1, N_Q_HEADS)), jnp.tile(sin, (1, N_Q_HEADS))


def _pack_w_in(w):
    o_q = 0
    o_k = o_q + ATTN_WIDTH
    o_v = o_k + KV_WIDTH
    o_conv = o_v + KV_WIDTH
    o_z = o_conv + 2 * CONV_WIDTH
    o_xbc = o_z + SSD_INNER
    o_dt = o_xbc + XBC_WIDTH
    o_g = o_dt + 2 * SSD_HEADS
    main = jnp.concatenate([w[:, o_xbc:o_dt], w[:, o_k:o_v], w[:, o_v:o_conv], w[:, o_conv:o_z],
                            w[:, o_g:o_g + N_BRANCH * D_MODEL], w[:, o_q:o_k], w[:, o_z:o_xbc]], axis=1).astype(BF16)
    dt = jnp.pad(w[:, o_dt:o_g], ((0, 0), (0, LANES - 2 * SSD_HEADS))).astype(BF16)
    return main, dt


def _lane_pad(v, lane0):
    return jnp.zeros((1, LANES), F32).at[0, lane0:lane0 + v.shape[0]].set(v)


def _state_to_kernel_layout(h):
    n = h.shape[0]
    h = h.reshape(n, SSD_GROUPS, SSD_HPG, SSD_HEAD_DIM, D_STATE)
    return jnp.transpose(h, (0, 1, 4, 2, 3)).reshape(n, SSD_GROUPS, D_STATE, SSD_HPG * SSD_HEAD_DIM)


def _state_from_kernel_layout(h):
    n = h.shape[0]
    h = h.reshape(n, SSD_GROUPS, D_STATE, SSD_HPG, SSD_HEAD_DIM)
    return jnp.transpose(h, (0, 1, 3, 4, 2)).reshape(n, SSD_HEADS, SSD_HEAD_DIM, D_STATE)


def kernel(x_prompt, x_sample, cache_k, cache_v, state_ssd, c, c_ctx, w_ada, b_ada, norm1_g, norm2_g, w_in, gate_b, attn_sink, w_attn_o, conv_dw_w, conv_dw_b, conv_ln_g, conv_ln_b, w_conv_o, ssd_conv_w, ssd_conv_b, ssd_A_log, ssd_dt_bias, ssd_D, ssd_norm_g, w_ssd_o, w_out, peer_w_q, peer_sub_keys, peer_u, peer_v, final_g):
    nctx, ctx_len, d = x_prompt.shape
    nlat, lat_len, _ = x_sample.shape
    depth = w_in.shape[0]
    ctx_rows = nctx * ctx_len
    lat_rows = nlat * lat_len
    rows = ctx_rows + lat_rows
    past = cache_k.shape[2]
    tm = 512

    def mod_row(tile_rows):
        def f(i):
            r = i * tile_rows
            return jnp.where(r < ctx_rows, 0, 1 + (r - ctx_rows) // lat_len)
        return f

    x = jnp.concatenate([x_prompt.reshape(ctx_rows, d), x_sample.reshape(lat_rows, d)], axis=0)

    cond = jnp.concatenate([c_ctx[None], c, jnp.zeros((8 - 1 - nlat, d), F32)], axis=0)
    mods = []
    for l in range(depth):
        mod = _matmul(cond, w_ada[l], tm=8, tn=1024, silu_in=True, bias=b_ada[l][None])
        mods.append(jnp.transpose(mod.reshape(8, 6, d), (1, 0, 2)).reshape(6, 8, 1, d))

    cos, sin = _rope_tables(lat_len)
    expand = jnp.repeat(jnp.eye(LANES, SSD_HEADS, dtype=F32), SSD_HEAD_DIM, axis=1)
    expand = (expand, jnp.roll(expand, SSD_HEADS, axis=0))
    h0_ctx = jnp.zeros((nctx, SSD_GROUPS, D_STATE, SSD_HPG * SSD_HEAD_DIM), F32)

    new_k, new_v, new_s = [], [], []
    for l in range(depth):
        shift1, scale1, gate1, shift2, scale2, gate2 = [mods[l][i] for i in range(6)]
        w_main, w_dt = _pack_w_in(w_in[l])

        (h,) = _norm_mod(x, norm1_g[l][None], shift1, scale1, mod_row(tm), tm=tm, out_dtypes=(BF16,))
        u = _matmul(h, w_main, tm=1024, tn=512)
        udt = _matmul(h, w_dt, tm=1024, tn=LANES)

        new_k.append(u[:ctx_rows, COL_K:COL_K + KV_WIDTH].reshape(nctx, ctx_len, N_KV_HEADS, HEAD_DIM))
        new_v.append(u[:ctx_rows, COL_V:COL_V + KV_WIDTH].reshape(nctx, ctx_len, N_KV_HEADS, HEAD_DIM))

        a_ctx = _ctx_attention(u, attn_sink[l], row0=0, nseq=nctx, length=ctx_len)
        q_rot, k_rot = _rope(u, cos, sin, row0=ctx_rows, nrows=lat_rows, length=lat_len, tm=tm)
        a_lat = _lat_attention(q_rot, k_rot, u, cache_k[:, l].reshape(nlat, past, KV_WIDTH),
                               cache_v[:, l].reshape(nlat, past, KV_WIDTH), attn_sink[l],
                               row0=ctx_rows, nseq=nlat, length=lat_len)
        attn = jnp.concatenate([a_ctx, a_lat], axis=0)

        conv = _conformer(u, conv_dw_w[l], conv_dw_b[l][None], conv_ln_g[l][None], conv_ln_b[l][None],
                          ctx_rows=ctx_rows, ctx_len=ctx_len, lat_len=lat_len, tb=ctx_len)

        ys, finals = [], []
        for dr in range(2):
            lane0 = dr * SSD_HEADS
            args = (ssd_conv_w[l, dr], ssd_conv_b[l, dr][None], _lane_pad(ssd_dt_bias[l, dr], lane0),
                    _lane_pad(-jnp.exp(ssd_A_log[l, dr]), lane0), expand[dr],
                    jnp.repeat(ssd_D[l, dr], SSD_HEAD_DIM)[None])
            y_c, f_c = _ssd_direction(u, udt, *args, h0_ctx, reverse=dr == 1, row0=0, nseq=nctx, length=ctx_len)
            y_l, _ = _ssd_direction(u, udt, *args, _state_to_kernel_layout(state_ssd[:, l, dr]),
                                    reverse=dr == 1, row0=ctx_rows, nseq=nlat, length=lat_len)
            ys.append(jnp.concatenate([y_c, y_l], axis=0))
            finals.append(_state_from_kernel_layout(f_c))
        new_s.append(jnp.stack(finals, axis=1))
        ssd = _ssd_gate(ys[0], ys[1], u, ssd_norm_g[l][None], tm=tm)

        merged = _merge(attn, conv, ssd, w_attn_o[l].astype(BF16), w_conv_o[l].astype(BF16), w_ssd_o[l].astype(BF16),
                        u, gate_b[l], tm=tm, tn=512)
        x = _matmul(merged, w_out[l].astype(BF16), tm=tm, tn=512, res=x, gate=gate1, gate_row=mod_row(tm))

        h2b, h2 = _norm_mod(x, norm2_g[l][None], shift2, scale2, mod_row(tm), tm=tm, out_dtypes=(BF16, F32))
        q = _matmul(h2b, peer_w_q[l].astype(BF16), tm=tm, tn=512)
        keys = peer_sub_keys[l].reshape(2 * PEER_HEADS, N_KEYS, PEER_HALF).astype(BF16)
        idx_t, w_t = _peer_topk(q, keys, tm=256)
        uv = jnp.concatenate([peer_u[l], peer_v[l]], axis=1)
        x = _peer_experts(idx_t.T, w_t, h2, x, gate2, mod_row(PEER_TOKENS), uv)

    y = _final_norm(x, final_g[None], tm=tm)
    y_prompt = y[:ctx_rows].reshape(nctx, ctx_len, d)
    y_sample = y[ctx_rows:].reshape(nlat, lat_len, d)
    return (y_prompt, y_sample, jnp.stack(new_k, axis=1), jnp.stack(new_v, axis=1), jnp.stack(new_s, axis=1))
```

```python
import functools
import math

import jax
import jax.numpy as jnp
from jax import lax
from jax.experimental import pallas as pl
from jax.experimental.pallas import tpu as pltpu

F32 = jnp.float32
BF16 = jnp.bfloat16
I32 = jnp.int32

D_MODEL = 2048
GRID_W = 64
BLOCK = 128
EPS = 1e-6
NEG_INF = -1e30
N_Q_HEADS = 16
N_KV_HEADS = 4
HEAD_DIM = 64
Q_PER_KV = N_Q_HEADS // N_KV_HEADS
ROPE_BASE = 10000.0
ROPE_FREQS = HEAD_DIM // 4
ATTN_SCALE = HEAD_DIM ** -0.5
ATTN_WIDTH = N_Q_HEADS * HEAD_DIM
KV_WIDTH = N_KV_HEADS * HEAD_DIM
CONV_WIDTH = 1024
CONV_KERNEL = 31
CONV_PAD = CONV_KERNEL // 2
CONV_HALO = 16
SSD_HEADS = 16
SSD_HEAD_DIM = 64
SSD_INNER = SSD_HEADS * SSD_HEAD_DIM
SSD_GROUPS = 2
SSD_HPG = SSD_HEADS // SSD_GROUPS
D_STATE = 128
SSD_CONV = 4
SSD_CHUNK = 128
XBC_WIDTH = SSD_INNER + 2 * SSD_GROUPS * D_STATE
PEER_HEADS = 8
N_KEYS = 128
N_EXPERTS = N_KEYS * N_KEYS
PEER_HALF = 128
PEER_TOPK = 16
PEER_SLOTS = PEER_HEADS * PEER_TOPK
N_BRANCH = 3
LANES = 128

COL_XBC = 0
COL_K = XBC_WIDTH
COL_V = COL_K + KV_WIDTH
COL_CONV = COL_V + KV_WIDTH
COL_GATES = COL_CONV + 2 * CONV_WIDTH
COL_Q = COL_GATES + N_BRANCH * D_MODEL
COL_Z = COL_Q + ATTN_WIDTH
U_WIDTH = COL_Z + SSD_INNER

VMEM_LIMIT = 48 * 1024 * 1024


def _cparams(*sem):
    return pltpu.CompilerParams(dimension_semantics=sem, vmem_limit_bytes=VMEM_LIMIT)


def _mm_kernel(*refs, silu_in, has_bias, has_res):
    x_ref, w_ref = refs[0], refs[1]
    pos = 2
    x = x_ref[...]
    if silu_in:
        x = x * jax.nn.sigmoid(x)
    acc = jnp.dot(x.astype(BF16), w_ref[...].astype(BF16), preferred_element_type=F32)
    if has_bias:
        acc = acc + refs[pos][...]
        pos += 1
    if has_res:
        acc = refs[pos][...] + refs[pos + 1][...] * acc
        pos += 2
    o_ref = refs[pos]
    o_ref[...] = acc.astype(o_ref.dtype)


def _matmul(x, w, *, tm, tn, name, out_dtype=F32, silu_in=False, bias=None, res=None, gate=None, gate_row=None):
    m, k = x.shape
    n = w.shape[1]
    in_specs = [pl.BlockSpec((tm, k), lambda i, j: (i, 0)), pl.BlockSpec((k, tn), lambda i, j: (0, j))]
    args = [x, w]
    if bias is not None:
        in_specs.append(pl.BlockSpec((1, tn), lambda i, j: (0, j)))
        args.append(bias)
    if res is not None:
        in_specs.append(pl.BlockSpec((tm, tn), lambda i, j: (i, j)))
        in_specs.append(pl.BlockSpec((None, 1, tn), lambda i, j: (gate_row(i), 0, j)))
        args += [res, gate]
    return pl.pallas_call(
        functools.partial(_mm_kernel, silu_in=silu_in, has_bias=bias is not None, has_res=res is not None),
        grid=(m // tm, n // tn),
        in_specs=in_specs,
        out_specs=pl.BlockSpec((tm, tn), lambda i, j: (i, j)),
        out_shape=jax.ShapeDtypeStruct((m, n), out_dtype),
        compiler_params=_cparams("parallel", "arbitrary"),
        name=name,
    )(*args)


def _norm_mod_kernel(x_ref, g_ref, shift_ref, scale_ref, *o_refs):
    x = x_ref[...]
    y = x * lax.rsqrt(jnp.mean(x * x, axis=-1, keepdims=True) + EPS)
    h = (y * g_ref[...]) * (1.0 + scale_ref[...]) + shift_ref[...]
    for o_ref in o_refs:
        o_ref[...] = h.astype(o_ref.dtype)


def _norm_mod(x, g, shift, scale, row_of_tile, *, tm, out_dtypes):
    m, d = x.shape
    mod_spec = pl.BlockSpec((None, 1, d), lambda i: (row_of_tile(i), 0, 0))
    outs = pl.pallas_call(
        _norm_mod_kernel,
        grid=(m // tm,),
        in_specs=[pl.BlockSpec((tm, d), lambda i: (i, 0)), pl.BlockSpec((1, d), lambda i: (0, 0)), mod_spec, mod_spec],
        out_specs=[pl.BlockSpec((tm, d), lambda i: (i, 0)) for _ in out_dtypes],
        out_shape=[jax.ShapeDtypeStruct((m, d), dt) for dt in out_dtypes],
        compiler_params=_cparams("parallel"),
        name="norm_mod",
    )(x, g, shift, scale)
    return outs


def _final_norm_kernel(x_ref, g_ref, o_ref):
    x = x_ref[...]
    o_ref[...] = x * lax.rsqrt(jnp.mean(x * x, axis=-1, keepdims=True) + EPS) * g_ref[...]


def _final_norm(x, g, *, tm):
    m, d = x.shape
    return pl.pallas_call(
        _final_norm_kernel,
        grid=(m // tm,),
        in_specs=[pl.BlockSpec((tm, d), lambda i: (i, 0)), pl.BlockSpec((1, d), lambda i: (0, 0))],
        out_specs=pl.BlockSpec((tm, d), lambda i: (i, 0)),
        out_shape=jax.ShapeDtypeStruct((m, d), F32),
        compiler_params=_cparams("parallel"),
        name="final_norm",
    )(x, g)


def _sink_column(sink_ref, kh, rows):
    grp = lax.broadcasted_iota(I32, (rows, 1), 0) // (rows // Q_PER_KV)
    col = jnp.full((rows, 1), sink_ref[kh * Q_PER_KV], F32)
    for g in range(1, Q_PER_KV):
        col = jnp.where(grp == g, sink_ref[kh * Q_PER_KV + g], col)
    return col


def _ctx_attn_kernel(sink_ref, q_ref, k_ref, v_ref, o_ref):
    length = q_ref.shape[0]
    for kh in range(N_KV_HEADS):
        ksl = slice(kh * HEAD_DIM, (kh + 1) * HEAD_DIM)
        kk = k_ref[:, ksl].astype(BF16)
        vv = v_ref[:, ksl].astype(BF16)
        qq = jnp.concatenate(
            [q_ref[:, (kh * Q_PER_KV + g) * HEAD_DIM:(kh * Q_PER_KV + g + 1) * HEAD_DIM] for g in range(Q_PER_KV)],
            axis=0).astype(BF16)
        s = lax.dot_general(qq, kk, (((1,), (1,)), ((), ())), preferred_element_type=F32) * ATTN_SCALE
        sink = _sink_column(sink_ref, kh, Q_PER_KV * length)
        m = jnp.maximum(jnp.max(s, axis=-1, keepdims=True), sink)
        p = jnp.exp(s - m)
        denom = jnp.sum(p, axis=-1, keepdims=True) + jnp.exp(sink - m)
        o = jnp.dot((p / denom).astype(BF16), vv, preferred_element_type=F32)
        for g in range(Q_PER_KV):
            h = kh * Q_PER_KV + g
            o_ref[:, h * HEAD_DIM:(h + 1) * HEAD_DIM] = o[g * length:(g + 1) * length].astype(o_ref.dtype)


def _ctx_attention(u, sink, *, row0, nseq, length):
    r0 = row0 // length
    return pl.pallas_call(
        _ctx_attn_kernel,
        grid=(nseq,),
        in_specs=[
            pl.BlockSpec(memory_space=pltpu.SMEM),
            pl.BlockSpec((length, ATTN_WIDTH), lambda b: (r0 + b, COL_Q // ATTN_WIDTH)),
            pl.BlockSpec((length, KV_WIDTH), lambda b: (r0 + b, COL_K // KV_WIDTH)),
            pl.BlockSpec((length, KV_WIDTH), lambda b: (r0 + b, COL_V // KV_WIDTH)),
        ],
        out_specs=pl.BlockSpec((length, ATTN_WIDTH), lambda b: (b, 0)),
        out_shape=jax.ShapeDtypeStruct((nseq * length, ATTN_WIDTH), BF16),
        compiler_params=_cparams("parallel"),
        name="ctx_attn",
    )(sink, u, u, u)


def _rope_kernel(q_ref, k_ref, cos_ref, sin_ref, qo_ref, ko_ref):
    def rot(x, cos, sin):
        width = x.shape[1]
        lane = lax.broadcasted_iota(I32, x.shape, 1)
        first = (lane // ROPE_FREQS) % 2 == 0
        partner = jnp.where(first, pltpu.roll(x, width - ROPE_FREQS, axis=1), pltpu.roll(x, ROPE_FREQS, axis=1))
        return x * cos + partner * sin

    qo_ref[...] = (rot(q_ref[...], cos_ref[...], sin_ref[...]) * ATTN_SCALE).astype(qo_ref.dtype)
    ko_ref[...] = rot(k_ref[...], cos_ref[:, :KV_WIDTH], sin_ref[:, :KV_WIDTH]).astype(ko_ref.dtype)


def _rope(u, cos, sin, *, row0, nrows, length, tm):
    r0 = row0 // tm
    per_seq = length // tm
    return pl.pallas_call(
        _rope_kernel,
        grid=(nrows // tm,),
        in_specs=[
            pl.BlockSpec((tm, ATTN_WIDTH), lambda i: (r0 + i, COL_Q // ATTN_WIDTH)),
            pl.BlockSpec((tm, KV_WIDTH), lambda i: (r0 + i, COL_K // KV_WIDTH)),
            pl.BlockSpec((tm, ATTN_WIDTH), lambda i: (i % per_seq, 0)),
            pl.BlockSpec((tm, ATTN_WIDTH), lambda i: (i % per_seq, 0)),
        ],
        out_specs=[pl.BlockSpec((tm, ATTN_WIDTH), lambda i: (i, 0)), pl.BlockSpec((tm, KV_WIDTH), lambda i: (i, 0))],
        out_shape=[jax.ShapeDtypeStruct((nrows, ATTN_WIDTH), BF16), jax.ShapeDtypeStruct((nrows, KV_WIDTH), BF16)],
        compiler_params=_cparams("parallel"),
        name="rope",
    )(u, u, cos, sin)


def _lat_attn_kernel(sink_ref, q_ref, kp_ref, kc_ref, kn_ref, vp_ref, vc_ref, vn_ref, ck_ref, cv_ref, o_ref):
    n = pl.program_id(1)
    nb = pl.num_programs(1)
    rows = Q_PER_KV * BLOCK
    r = lax.broadcasted_iota(I32, (rows, 3 * BLOCK), 0) % BLOCK
    c = lax.broadcasted_iota(I32, (rows, 3 * BLOCK), 1)
    lo = jnp.where(n > 0, 0, BLOCK)
    hi = jnp.where(n < nb - 1, 3 * BLOCK, 2 * BLOCK)
    mask = (c >= r) & (c <= r + 2 * BLOCK) & (c >= lo) & (c < hi)
    for kh in range(N_KV_HEADS):
        ksl = slice(kh * HEAD_DIM, (kh + 1) * HEAD_DIM)
        kk = jnp.concatenate([kp_ref[:, ksl], kc_ref[:, ksl], kn_ref[:, ksl], ck_ref[:, ksl].astype(BF16)], axis=0)
        vv = jnp.concatenate([vp_ref[:, ksl], vc_ref[:, ksl], vn_ref[:, ksl], cv_ref[:, ksl]], axis=0).astype(BF16)
        qq = jnp.concatenate(
            [q_ref[:, (kh * Q_PER_KV + g) * HEAD_DIM:(kh * Q_PER_KV + g + 1) * HEAD_DIM] for g in range(Q_PER_KV)],
            axis=0)
        s = lax.dot_general(qq, kk, (((1,), (1,)), ((), ())), preferred_element_type=F32)
        s_win = jnp.where(mask, s[:, :3 * BLOCK], NEG_INF)
        s_ctx = s[:, 3 * BLOCK:]
        sink = _sink_column(sink_ref, kh, rows)
        m = jnp.maximum(jnp.maximum(jnp.max(s_win, axis=-1, keepdims=True), jnp.max(s_ctx, axis=-1, keepdims=True)),
                        sink)
        p_win = jnp.exp(s_win - m)
        p_ctx = jnp.exp(s_ctx - m)
        denom = (jnp.sum(p_win, axis=-1, keepdims=True) + jnp.sum(p_ctx, axis=-1, keepdims=True)
                 + jnp.exp(sink - m))
        inv = 1.0 / denom
        p = jnp.concatenate([p_win * inv, p_ctx * inv], axis=1).astype(BF16)
        o = jnp.dot(p, vv, preferred_element_type=F32)
        for g in range(Q_PER_KV):
            h = kh * Q_PER_KV + g
            o_ref[:, h * HEAD_DIM:(h + 1) * HEAD_DIM] = o[g * BLOCK:(g + 1) * BLOCK].astype(o_ref.dtype)


def _lat_attention(q_rot, k_rot, u, ck, cv, sink, *, row0, nseq, length):
    nb = length // BLOCK
    r0 = row0 // BLOCK

    def prev(b, n):
        return jnp.maximum(n - 1, 0)

    def nxt(b, n):
        return jnp.minimum(n + 1, nb - 1)

    vcol = COL_V // KV_WIDTH
    past = ck.shape[1]
    return pl.pallas_call(
        _lat_attn_kernel,
        grid=(nseq, nb),
        in_specs=[
            pl.BlockSpec(memory_space=pltpu.SMEM),
            pl.BlockSpec((BLOCK, ATTN_WIDTH), lambda b, n: (b * nb + n, 0)),
            pl.BlockSpec((BLOCK, KV_WIDTH), lambda b, n: (b * nb + prev(b, n), 0)),
            pl.BlockSpec((BLOCK, KV_WIDTH), lambda b, n: (b * nb + n, 0)),
            pl.BlockSpec((BLOCK, KV_WIDTH), lambda b, n: (b * nb + nxt(b, n), 0)),
            pl.BlockSpec((BLOCK, KV_WIDTH), lambda b, n: (r0 + b * nb + prev(b, n), vcol)),
            pl.BlockSpec((BLOCK, KV_WIDTH), lambda b, n: (r0 + b * nb + n, vcol)),
            pl.BlockSpec((BLOCK, KV_WIDTH), lambda b, n: (r0 + b * nb + nxt(b, n), vcol)),
            pl.BlockSpec((None, past, KV_WIDTH), lambda b, n: (b, 0, 0)),
            pl.BlockSpec((None, past, KV_WIDTH), lambda b, n: (b, 0, 0)),
        ],
        out_specs=pl.BlockSpec((BLOCK, ATTN_WIDTH), lambda b, n: (b * nb + n, 0)),
        out_shape=jax.ShapeDtypeStruct((nseq * length, ATTN_WIDTH), BF16),
        compiler_params=_cparams("parallel", "arbitrary"),
        name="lat_attn",
    )(sink, q_rot, k_rot, k_rot, k_rot, u, u, u, ck, cv)


def _conv_kernel(prev_ref, cur_ref, next_ref, w_ref, b_ref, lng_ref, lnb_ref, o_ref, hp_ref, *, ctx_blocks, per_seq):
    i = pl.program_id(0)
    tb = cur_ref.shape[0]
    j = (i - ctx_blocks) % per_seq
    is_lat = i >= ctx_blocks
    has_prev = jnp.logical_and(is_lat, j != 0)
    has_next = jnp.logical_and(is_lat, j != per_seq - 1)

    def glu(x):
        return x[:, :CONV_WIDTH] * jax.nn.sigmoid(x[:, CONV_WIDTH:])

    hp_ref[0:CONV_HALO, :] = jnp.where(has_prev, glu(prev_ref[...]), 0.0)
    hp_ref[CONV_HALO:CONV_HALO + tb, :] = glu(cur_ref[...])
    hp_ref[CONV_HALO + tb:2 * CONV_HALO + tb, :] = jnp.where(has_next, glu(next_ref[...]), 0.0)
    acc = jnp.zeros((tb, CONV_WIDTH), F32)
    off = CONV_HALO - CONV_PAD
    for t in range(CONV_KERNEL):
        acc = acc + w_ref[t:t + 1, :] * hp_ref[off + t:off + t + tb, :]
    acc = acc + b_ref[...]
    mu = jnp.mean(acc, axis=-1, keepdims=True)
    xc = acc - mu
    var = jnp.mean(xc * xc, axis=-1, keepdims=True)
    y = xc * lax.rsqrt(var + EPS) * lng_ref[...] + lnb_ref[...]
    o_ref[...] = (y * jax.nn.sigmoid(y)).astype(o_ref.dtype)


def _conformer(u, dw_w, dw_b, ln_g, ln_b, *, ctx_rows, ctx_len, lat_len, tb):
    assert ctx_len == tb
    m = u.shape[0]
    nblk = m // tb
    ctx_blocks = ctx_rows // tb
    per_seq = lat_len // tb
    hb = tb // CONV_HALO
    ccol = COL_CONV // (2 * CONV_WIDTH)
    last = m // CONV_HALO - 1
    return pl.pallas_call(
        functools.partial(_conv_kernel, ctx_blocks=ctx_blocks, per_seq=per_seq),
        grid=(nblk,),
        in_specs=[
            pl.BlockSpec((CONV_HALO, 2 * CONV_WIDTH), lambda i: (jnp.maximum(i * hb - 1, 0), ccol)),
            pl.BlockSpec((tb, 2 * CONV_WIDTH), lambda i: (i, ccol)),
            pl.BlockSpec((CONV_HALO, 2 * CONV_WIDTH), lambda i: (jnp.minimum((i + 1) * hb, last), ccol)),
            pl.BlockSpec((CONV_KERNEL, CONV_WIDTH), lambda i: (0, 0)),
            pl.BlockSpec((1, CONV_WIDTH), lambda i: (0, 0)),
            pl.BlockSpec((1, CONV_WIDTH), lambda i: (0, 0)),
            pl.BlockSpec((1, CONV_WIDTH), lambda i: (0, 0)),
        ],
        out_specs=pl.BlockSpec((tb, CONV_WIDTH), lambda i: (i, 0)),
        out_shape=jax.ShapeDtypeStruct((m, CONV_WIDTH), BF16),
        scratch_shapes=[pltpu.VMEM((tb + 2 * CONV_HALO, CONV_WIDTH), F32)],
        compiler_params=_cparams("parallel"),
        name="conformer",
    )(u, u, u, dw_w, dw_b, ln_g, ln_b)


def _exact_dot(a, b):
    return jnp.dot(a, b, precision=lax.Precision.HIGHEST, preferred_element_type=F32)


def _ssd_kernel(xbc_ref, dt_ref, cw_ref, cb_ref, dtb_ref, a_ref, e_ref, dskip_ref, h0_ref, y_ref, st_ref,
                ext_ref, *, reverse, lane0):
    c = pl.program_id(1)
    L = SSD_CHUNK
    pad = 8
    cur = xbc_ref[...]

    @pl.when(c == 0)
    def _():
        st_ref[...] = h0_ref[...]
        ext_ref[...] = jnp.zeros_like(ext_ref)

    ext_ref[pad:pad + L, :] = cur
    conv = jnp.zeros((L, XBC_WIDTH), F32)
    for j in range(SSD_CONV):
        off = pad + (SSD_CONV - 1 - j) if reverse else pad - (SSD_CONV - 1) + j
        conv = conv + cw_ref[j:j + 1, :] * ext_ref[off:off + L, :]
    conv = conv + cb_ref[...]
    if reverse:
        ext_ref[pad + L:pad + L + pad, :] = cur[0:pad, :]
    else:
        ext_ref[0:pad, :] = cur[L - pad:L, :]
    act = conv * jax.nn.sigmoid(conv)
    xs = act[:, :SSD_INNER]

    dt = jax.nn.softplus(dt_ref[...] + dtb_ref[...])
    da = dt * a_ref[...]
    ti = lax.broadcasted_iota(I32, (L, L), 0)
    si = lax.broadcasted_iota(I32, (L, L), 1)
    tri = (si >= ti) if reverse else (si <= ti)
    acum = _exact_dot(tri.astype(F32), da)
    acum_t = acum.T
    total = acum[0:1, :] if reverse else acum[L - 1:L, :]
    to_end = jnp.exp(total - acum)
    fac = jnp.concatenate([dt, dt * to_end, jnp.exp(acum), jnp.broadcast_to(jnp.exp(total), (pad, LANES))], axis=0)
    fac = _exact_dot(fac, e_ref[...])
    xg = (xs * fac[0:L]).astype(BF16)
    xg_end = (xs * fac[L:2 * L]).astype(BF16)
    eac = fac[2 * L:3 * L]
    dec = fac[3 * L:3 * L + 1]

    y_ref[...] = dskip_ref[...] * xs
    gw = SSD_HPG * SSD_HEAD_DIM
    for g in range(SSD_GROUPS):
        bg = act[:, SSD_INNER + g * D_STATE:SSD_INNER + (g + 1) * D_STATE]
        cg = act[:, SSD_INNER + SSD_GROUPS * D_STATE + g * D_STATE:SSD_INNER + SSD_GROUPS * D_STATE + (g + 1) * D_STATE]
        bgb = bg.astype(BF16)
        cgb = cg.astype(BF16)
        cbm = lax.dot_general(cgb, bgb, (((1,), (1,)), ((), ())), preferred_element_type=F32)
        st = st_ref[g]
        y_off = jnp.dot(cgb, st.astype(BF16), preferred_element_type=F32) * eac[:, g * gw:(g + 1) * gw]
        y_ref[:, g * gw:(g + 1) * gw] += y_off
        st_ref[g] = st * dec[:, g * gw:(g + 1) * gw] + jnp.dot(
            bg.T.astype(BF16), xg_end[:, g * gw:(g + 1) * gw], preferred_element_type=F32)
        for hh in range(SSD_HPG):
            h = g * SSD_HPG + hh
            lane = lane0 + h
            seg = acum[:, lane:lane + 1] - acum_t[lane:lane + 1, :]
            decay = jnp.exp(jnp.where(tri, seg, -jnp.inf))
            mat = (cbm * decay).astype(BF16)
            y_ref[:, h * SSD_HEAD_DIM:(h + 1) * SSD_HEAD_DIM] += jnp.dot(
                mat, xg[:, h * SSD_HEAD_DIM:(h + 1) * SSD_HEAD_DIM], preferred_element_type=F32)


def _ssd_direction(u, udt, conv_w, conv_b, dt_bias, a_neg, expand, dskip, h0, *, reverse, row0, nseq, length):
    nc = length // SSD_CHUNK
    r0 = row0 // SSD_CHUNK

    def chunk(s, c):
        return r0 + s * nc + (nc - 1 - c if reverse else c)

    return pl.pallas_call(
        functools.partial(_ssd_kernel, reverse=reverse, lane0=SSD_HEADS if reverse else 0),
        grid=(nseq, nc),
        in_specs=[
            pl.BlockSpec((SSD_CHUNK, XBC_WIDTH), lambda s, c: (chunk(s, c), COL_XBC // XBC_WIDTH)),
            pl.BlockSpec((SSD_CHUNK, LANES), lambda s, c: (chunk(s, c), 0)),
            pl.BlockSpec((SSD_CONV, XBC_WIDTH), lambda s, c: (0, 0)),
            pl.BlockSpec((1, XBC_WIDTH), lambda s, c: (0, 0)),
            pl.BlockSpec((1, LANES), lambda s, c: (0, 0)),
            pl.BlockSpec((1, LANES), lambda s, c: (0, 0)),
            pl.BlockSpec((LANES, SSD_INNER), lambda s, c: (0, 0)),
            pl.BlockSpec((1, SSD_INNER), lambda s, c: (0, 0)),
            pl.BlockSpec((None, SSD_GROUPS, D_STATE, SSD_HPG * SSD_HEAD_DIM), lambda s, c: (s, 0, 0, 0)),
        ],
        out_specs=[
            pl.BlockSpec((SSD_CHUNK, SSD_INNER), lambda s, c: (chunk(s, c) - r0, 0)),
            pl.BlockSpec((None, SSD_GROUPS, D_STATE, SSD_HPG * SSD_HEAD_DIM), lambda s, c: (s, 0, 0, 0)),
        ],
        out_shape=[
            jax.ShapeDtypeStruct((nseq * length, SSD_INNER), F32),
            jax.ShapeDtypeStruct((nseq, SSD_GROUPS, D_STATE, SSD_HPG * SSD_HEAD_DIM), F32),
        ],
        scratch_shapes=[pltpu.VMEM((SSD_CHUNK + 16, XBC_WIDTH), F32)],
        compiler_params=_cparams("parallel", "arbitrary"),
        name="ssd_bwd" if reverse else "ssd_fwd",
    )(u, udt, conv_w, conv_b, dt_bias, a_neg, expand, dskip, h0)


def _ssd_gate_kernel(yf_ref, yb_ref, z_ref, g_ref, o_ref):
    z = z_ref[...]
    y = (yf_ref[...] + yb_ref[...]) * (z * jax.nn.sigmoid(z))
    o_ref[...] = (y * lax.rsqrt(jnp.mean(y * y, axis=-1, keepdims=True) + EPS) * g_ref[...]).astype(o_ref.dtype)


def _ssd_gate(yf, yb, u, norm_g, *, tm):
    m = yf.shape[0]
    return pl.pallas_call(
        _ssd_gate_kernel,
        grid=(m // tm,),
        in_specs=[
            pl.BlockSpec((tm, SSD_INNER), lambda i: (i, 0)),
            pl.BlockSpec((tm, SSD_INNER), lambda i: (i, 0)),
            pl.BlockSpec((tm, SSD_INNER), lambda i: (i, COL_Z // SSD_INNER)),
            pl.BlockSpec((1, SSD_INNER), lambda i: (0, 0)),
        ],
        out_specs=pl.BlockSpec((tm, SSD_INNER), lambda i: (i, 0)),
        out_shape=jax.ShapeDtypeStruct((m, SSD_INNER), BF16),
        compiler_params=_cparams("parallel"),
        name="ssd_gate",
    )(yf, yb, u, norm_g)


def _merge_kernel(a_ref, c_ref, s_ref, wa_ref, wc_ref, ws_ref, g0_ref, g1_ref, g2_ref, gb_ref, o_ref):
    acc = None
    for b, (x_ref, w_ref, g_ref) in enumerate(((a_ref, wa_ref, g0_ref), (c_ref, wc_ref, g1_ref), (s_ref, ws_ref, g2_ref))):
        gate = jax.nn.sigmoid(g_ref[...] + gb_ref[b:b + 1, :])
        term = gate * jnp.dot(x_ref[...], w_ref[...], preferred_element_type=F32)
        acc = term if acc is None else acc + term
    o_ref[...] = acc.astype(o_ref.dtype)


def _merge(attn, conv, ssd, w_a, w_c, w_s, u, gate_b, *, tm, tn):
    m = attn.shape[0]
    width = attn.shape[1]
    gcol = COL_GATES // tn
    per = D_MODEL // tn
    act_spec = pl.BlockSpec((tm, width), lambda i, j: (i, 0))
    w_spec = pl.BlockSpec((width, tn), lambda i, j: (0, j))
    return pl.pallas_call(
        _merge_kernel,
        grid=(m // tm, D_MODEL // tn),
        in_specs=[act_spec, act_spec, act_spec, w_spec, w_spec, w_spec,
                  pl.BlockSpec((tm, tn), lambda i, j: (i, gcol + j)),
                  pl.BlockSpec((tm, tn), lambda i, j: (i, gcol + per + j)),
                  pl.BlockSpec((tm, tn), lambda i, j: (i, gcol + 2 * per + j)),
                  pl.BlockSpec((N_BRANCH, tn), lambda i, j: (0, j))],
        out_specs=pl.BlockSpec((tm, tn), lambda i, j: (i, j)),
        out_shape=jax.ShapeDtypeStruct((m, D_MODEL), BF16),
        compiler_params=_cparams("parallel", "arbitrary"),
        name="merge",
    )(attn, conv, ssd, w_a, w_c, w_s, u, u, u, gate_b)


def _top16(s, key=None, payload=None):
    if key is None:
        key = lax.broadcasted_iota(I32, s.shape, 0).astype(F32)
    vals, poss, pays = [], [], []
    for _ in range(PEER_TOPK):
        m = jnp.max(s, axis=0, keepdims=True)
        p = jnp.min(jnp.where(s == m, key, float(N_EXPERTS)), axis=0, keepdims=True)
        hit = key == p
        vals.append(m)
        poss.append(p)
        if payload is not None:
            pays.append(jnp.sum(jnp.where(hit, payload, 0.0), axis=0, keepdims=True))
        s = jnp.where(hit, -jnp.inf, s)
    vals = jnp.concatenate(vals, axis=0)
    if payload is not None:
        return vals, jnp.concatenate(pays, axis=0)
    return vals, jnp.concatenate(poss, axis=0)


def _peer_topk_kernel(q_ref, keys_ref, idx_ref, w_ref):
    for h in range(PEER_HEADS):
        halves = []
        for half in range(2):
            j = h * 2 + half
            qh = q_ref[:, j * PEER_HALF:(j + 1) * PEER_HALF].astype(BF16)
            st = lax.dot_general(keys_ref[j], qh, (((1,), (1,)), ((), ())), preferred_element_type=F32)
            halves.append(_top16(st))
        (v1, p1), (v2, p2) = halves
        sub = lax.broadcasted_iota(I32, (8, v1.shape[1]), 0).astype(F32)
        cand, flat, cidx = [], [], []
        for a in range(8):
            cand.append(v1[a:a + 1, :] + v2[0:8, :])
            flat.append(sub + float(a * PEER_TOPK))
            cidx.append(p1[a:a + 1, :] * N_KEYS + p2[0:8, :])
        cand.append(v1[0:1, :] + v2[8:16, :])
        flat.append(sub + 8.0)
        cidx.append(p1[0:1, :] * N_KEYS + p2[8:16, :])
        cand.append(v1[8:16, :] + v2[0:1, :])
        flat.append((sub + 8.0) * PEER_TOPK)
        cidx.append(p1[8:16, :] * N_KEYS + p2[0:1, :])
        top, idx = _top16(jnp.concatenate(cand, axis=0), jnp.concatenate(flat, axis=0), jnp.concatenate(cidx, axis=0))
        e = jnp.exp(top - top[0:1, :])
        w = e / jnp.sum(e, axis=0, keepdims=True)
        idx_ref[h * PEER_TOPK:(h + 1) * PEER_TOPK, :] = idx.astype(I32)
        w_ref[h * PEER_TOPK:(h + 1) * PEER_TOPK, :] = w


def _peer_topk(q, keys, *, tm):
    m = q.shape[0]
    return pl.pallas_call(
        _peer_topk_kernel,
        grid=(m // tm,),
        in_specs=[pl.BlockSpec((tm, q.shape[1]), lambda i: (i, 0)),
                  pl.BlockSpec(keys.shape, lambda i: (0, 0, 0))],
        out_specs=[pl.BlockSpec((PEER_SLOTS, tm), lambda i: (0, i)), pl.BlockSpec((PEER_SLOTS, tm), lambda i: (0, i))],
        out_shape=[jax.ShapeDtypeStruct((PEER_SLOTS, m), I32), jax.ShapeDtypeStruct((PEER_SLOTS, m), F32)],
        compiler_params=_cparams("parallel"),
        name="peer_topk",
    )(q, keys)


PEER_TOKENS = 128
PEER_BUFS = 8
PEER_AHEAD = PEER_BUFS - 1


def _peer_expert_kernel(idx_ref, idx_next_ref, wt_ref, h_ref, x_ref, gate_ref, uv_ref, o_ref, gbuf, sems):
    step = pl.program_id(0)
    last_step = pl.num_programs(0) - 1
    panels = D_MODEL // LANES
    per_piece = PEER_SLOTS // (2 * panels)
    lane = lax.broadcasted_iota(I32, (PEER_SLOTS, PEER_TOKENS), 1)

    def rows_copy(slot):
        return pltpu.make_async_copy(gbuf.at[slot], gbuf.at[slot], sems.at[slot])

    def start_rows(ids_ref, row, slot, lo, hi):
        for j in range(lo, hi):
            pltpu.make_async_copy(uv_ref.at[ids_ref[row, j]], gbuf.at[slot, :, j, :], sems.at[slot]).start(priority=j % 2)

    def token(t, issue):
        slot = t % PEER_BUFS
        rows_copy(slot).wait()
        xb = h_ref[pl.ds(t, 1), :]
        acc = jnp.zeros((PEER_SLOTS, LANES), F32)
        for g in range(panels):
            issue(g * per_piece, (g + 1) * per_piece)
            u_g = lax.bitcast_convert_type(gbuf[slot, g] & jnp.uint32(0xFFFF0000), F32)
            acc = acc + u_g * xb[:, g * LANES:(g + 1) * LANES]
        a = jnp.sum(acc, axis=1, keepdims=True)
        w = jnp.sum(jnp.where(lane == t, wt_ref[...], 0.0), axis=1, keepdims=True)
        coef = w * (0.5 * a * (1.0 + lax.erf(a * (1.0 / math.sqrt(2.0)))))
        out = []
        for g in range(panels):
            issue((panels + g) * per_piece, (panels + g + 1) * per_piece)
            v_g = lax.bitcast_convert_type(gbuf[slot, g] << 16, F32)
            out.append(jnp.sum(coef * v_g, axis=0, keepdims=True))
        out = jnp.concatenate(out, axis=1)
        o_ref[pl.ds(t, 1), :] = x_ref[pl.ds(t, 1), :] + gate_ref[...] * out

    @pl.when(step == 0)
    def _():
        for t in range(PEER_AHEAD):
            start_rows(idx_ref, t, t, 0, PEER_SLOTS)

    def body_same_step(t, carry):
        ahead = t + PEER_AHEAD
        token(t, functools.partial(start_rows, idx_ref, ahead, ahead % PEER_BUFS))
        return carry

    def body_next_step(t, carry):
        ahead = t + PEER_AHEAD
        token(t, functools.partial(start_rows, idx_next_ref, ahead - PEER_TOKENS, ahead % PEER_BUFS))
        return carry

    def body_drain(t, carry):
        token(t, lambda lo, hi: None)
        return carry

    lax.fori_loop(0, PEER_TOKENS - PEER_AHEAD, body_same_step, 0)

    @pl.when(step < last_step)
    def _():
        lax.fori_loop(PEER_TOKENS - PEER_AHEAD, PEER_TOKENS, body_next_step, 0)

    @pl.when(step == last_step)
    def _():
        lax.fori_loop(PEER_TOKENS - PEER_AHEAD, PEER_TOKENS, body_drain, 0)


def _pack_uv(u, v):
    hi = lax.bitcast_convert_type(u.astype(jnp.bfloat16), jnp.uint16).astype(jnp.uint32) << 16
    lo = lax.bitcast_convert_type(v.astype(jnp.bfloat16), jnp.uint16).astype(jnp.uint32)
    return (hi | lo).reshape(u.shape[0], u.shape[1] // LANES, LANES)


def _peer_experts(idx, wt, h, x, gate, gate_row, uv):
    m = h.shape[0]
    steps = m // PEER_TOKENS
    return pl.pallas_call(
        _peer_expert_kernel,
        grid=(steps,),
        in_specs=[
            pl.BlockSpec((PEER_TOKENS, PEER_SLOTS), lambda i: (i, 0), memory_space=pltpu.SMEM),
            pl.BlockSpec((PEER_TOKENS, PEER_SLOTS), lambda i: (jnp.minimum(i + 1, steps - 1), 0),
                         memory_space=pltpu.SMEM),
            pl.BlockSpec((PEER_SLOTS, PEER_TOKENS), lambda i: (0, i)),
            pl.BlockSpec((PEER_TOKENS, D_MODEL), lambda i: (i, 0)),
            pl.BlockSpec((PEER_TOKENS, D_MODEL), lambda i: (i, 0)),
            pl.BlockSpec((None, 1, D_MODEL), lambda i: (gate_row(i), 0, 0)),
            pl.BlockSpec(memory_space=pl.ANY),
        ],
        out_specs=pl.BlockSpec((PEER_TOKENS, D_MODEL), lambda i: (i, 0)),
        out_shape=jax.ShapeDtypeStruct((m, D_MODEL), F32),
        scratch_shapes=[pltpu.VMEM((PEER_BUFS, D_MODEL // LANES, PEER_SLOTS, LANES), jnp.uint32),
                        pltpu.SemaphoreType.DMA((PEER_BUFS,))],
        compiler_params=_cparams("arbitrary"),
        name="peer_experts",
    )(idx, idx, wt, h, x, gate, uv)


def _rope_tables(length):
    rows = length // GRID_W
    row = jnp.repeat(jnp.arange(rows, dtype=F32), GRID_W)
    col = jnp.tile(jnp.arange(GRID_W, dtype=F32), rows)
    freqs = ROPE_BASE ** (-jnp.arange(ROPE_FREQS, dtype=F32) / ROPE_FREQS)
    ar, ac = row[:, None] * freqs, col[:, None] * freqs
    cos = jnp.concatenate([jnp.cos(ar), jnp.cos(ar), jnp.cos(ac), jnp.cos(ac)], axis=-1)
    sin = jnp.concatenate([-jnp.sin(ar), jnp.sin(ar), -jnp.sin(ac), jnp.sin(ac)], axis=-1)
    return jnp.tile(cos, (1, N_Q_HEADS)), jnp.tile(sin, (1, N_Q_HEADS))


def _pack_w_in(w):
    o_q = 0
    o_k = o_q + ATTN_WIDTH
    o_v = o_k + KV_WIDTH
    o_conv = o_v + KV_WIDTH
    o_z = o_conv + 2 * CONV_WIDTH
    o_xbc = o_z + SSD_INNER
    o_dt = o_xbc + XBC_WIDTH
    o_g = o_dt + 2 * SSD_HEADS
    main = jnp.concatenate([w[:, o_xbc:o_dt], w[:, o_k:o_v], w[:, o_v:o_conv], w[:, o_conv:o_z],
                            w[:, o_g:o_g + N_BRANCH * D_MODEL], w[:, o_q:o_k], w[:, o_z:o_xbc]], axis=1).astype(BF16)
    dt = jnp.pad(w[:, o_dt:o_g], ((0, 0), (0, LANES - 2 * SSD_HEADS))).astype(BF16)
    return main, dt


def _lane_pad(v, lane0):
    return jnp.zeros((1, LANES), F32).at[0, lane0:lane0 + v.shape[0]].set(v)


def _state_to_kernel_layout(h):
    n = h.shape[0]
    h = h.reshape(n, SSD_GROUPS, SSD_HPG, SSD_HEAD_DIM, D_STATE)
    return jnp.transpose(h, (0, 1, 4, 2, 3)).reshape(n, SSD_GROUPS, D_STATE, SSD_HPG * SSD_HEAD_DIM)


def _state_from_kernel_layout(h):
    n = h.shape[0]
    h = h.reshape(n, SSD_GROUPS, D_STATE, SSD_HPG, SSD_HEAD_DIM)
    return jnp.transpose(h, (0, 1, 3, 4, 2)).reshape(n, SSD_HEADS, SSD_HEAD_DIM, D_STATE)


def kernel(x_prompt, x_sample, cache_k, cache_v, state_ssd, c, c_ctx, w_ada, b_ada, norm1_g, norm2_g, w_in, gate_b, attn_sink, w_attn_o, conv_dw_w, conv_dw_b, conv_ln_g, conv_ln_b, w_conv_o, ssd_conv_w, ssd_conv_b, ssd_A_log, ssd_dt_bias, ssd_D, ssd_norm_g, w_ssd_o, w_out, peer_w_q, peer_sub_keys, peer_u, peer_v, final_g):
    nctx, ctx_len, d = x_prompt.shape
    nlat, lat_len, _ = x_sample.shape
    depth = w_in.shape[0]
    ctx_rows = nctx * ctx_len
    lat_rows = nlat * lat_len
    rows = ctx_rows + lat_rows
    past = cache_k.shape[2]
    tm = 512

    def mod_row(tile_rows):
        def f(i):
            r = i * tile_rows
            return jnp.where(r < ctx_rows, 0, 1 + (r - ctx_rows) // lat_len)
        return f

    x = jnp.concatenate([x_prompt.reshape(ctx_rows, d), x_sample.reshape(lat_rows, d)], axis=0)

    cond = jnp.concatenate([c_ctx[None], c, jnp.zeros((8 - 1 - nlat, d), F32)], axis=0)
    mods = []
    for l in range(depth):
        mod = _matmul(cond, w_ada[l], tm=8, tn=1024, name="mm_ada", silu_in=True, bias=b_ada[l][None])
        mods.append(jnp.transpose(mod.reshape(8, 6, d), (1, 0, 2)).reshape(6, 8, 1, d))

    cos, sin = _rope_tables(lat_len)
    expand = jnp.repeat(jnp.eye(LANES, SSD_HEADS, dtype=F32), SSD_HEAD_DIM, axis=1)
    expand = (expand, jnp.roll(expand, SSD_HEADS, axis=0))
    h0_ctx = jnp.zeros((nctx, SSD_GROUPS, D_STATE, SSD_HPG * SSD_HEAD_DIM), F32)

    new_k, new_v, new_s = [], [], []
    for l in range(depth):
        shift1, scale1, gate1, shift2, scale2, gate2 = [mods[l][i] for i in range(6)]
        w_main, w_dt = _pack_w_in(w_in[l])

        (h,) = _norm_mod(x, norm1_g[l][None], shift1, scale1, mod_row(tm), tm=tm, out_dtypes=(BF16,))
        u = _matmul(h, w_main, tm=1024, tn=512, name="mm_in")
        udt = _matmul(h, w_dt, tm=1024, tn=LANES, name="mm_dt")

        new_k.append(u[:ctx_rows, COL_K:COL_K + KV_WIDTH].reshape(nctx, ctx_len, N_KV_HEADS, HEAD_DIM))
        new_v.append(u[:ctx_rows, COL_V:COL_V + KV_WIDTH].reshape(nctx, ctx_len, N_KV_HEADS, HEAD_DIM))

        a_ctx = _ctx_attention(u, attn_sink[l], row0=0, nseq=nctx, length=ctx_len)
        q_rot, k_rot = _rope(u, cos, sin, row0=ctx_rows, nrows=lat_rows, length=lat_len, tm=tm)
        a_lat = _lat_attention(q_rot, k_rot, u, cache_k[:, l].reshape(nlat, past, KV_WIDTH),
                               cache_v[:, l].reshape(nlat, past, KV_WIDTH), attn_sink[l],
                               row0=ctx_rows, nseq=nlat, length=lat_len)
        attn = jnp.concatenate([a_ctx, a_lat], axis=0)

        conv = _conformer(u, conv_dw_w[l], conv_dw_b[l][None], conv_ln_g[l][None], conv_ln_b[l][None],
                          ctx_rows=ctx_rows, ctx_len=ctx_len, lat_len=lat_len, tb=ctx_len)

        ys, finals = [], []
        for dr in range(2):
            lane0 = dr * SSD_HEADS
            args = (ssd_conv_w[l, dr], ssd_conv_b[l, dr][None], _lane_pad(ssd_dt_bias[l, dr], lane0),
                    _lane_pad(-jnp.exp(ssd_A_log[l, dr]), lane0), expand[dr],
                    jnp.repeat(ssd_D[l, dr], SSD_HEAD_DIM)[None])
            y_c, f_c = _ssd_direction(u, udt, *args, h0_ctx, reverse=dr == 1, row0=0, nseq=nctx, length=ctx_len)
            y_l, _ = _ssd_direction(u, udt, *args, _state_to_kernel_layout(state_ssd[:, l, dr]),
                                    reverse=dr == 1, row0=ctx_rows, nseq=nlat, length=lat_len)
            ys.append(jnp.concatenate([y_c, y_l], axis=0))
            finals.append(_state_from_kernel_layout(f_c))
        new_s.append(jnp.stack(finals, axis=1))
        ssd = _ssd_gate(ys[0], ys[1], u, ssd_norm_g[l][None], tm=tm)

        merged = _merge(attn, conv, ssd, w_attn_o[l].astype(BF16), w_conv_o[l].astype(BF16), w_ssd_o[l].astype(BF16),
                        u, gate_b[l], tm=tm, tn=512)
        x = _matmul(merged, w_out[l].astype(BF16), tm=tm, tn=512, name="mm_out", res=x, gate=gate1,
                    gate_row=mod_row(tm))

        h2b, h2 = _norm_mod(x, norm2_g[l][None], shift2, scale2, mod_row(tm), tm=tm, out_dtypes=(BF16, F32))
        q = _matmul(h2b, peer_w_q[l].astype(BF16), tm=tm, tn=512, name="mm_peer_q")
        keys = peer_sub_keys[l].reshape(2 * PEER_HEADS, N_KEYS, PEER_HALF).astype(BF16)
        idx_t, w_t = _peer_topk(q, keys, tm=256)
        uv = _pack_uv(peer_u[l], peer_v[l])
        x = _peer_experts(idx_t.T, w_t, h2, x, gate2, mod_row(PEER_TOKENS), uv)

    y = _final_norm(x, final_g[None], tm=tm)
    y_prompt = y[:ctx_rows].reshape(nctx, ctx_len, d)
    y_sample = y[ctx_rows:].reshape(nlat, lat_len, d)
    return (y_prompt, y_sample, jnp.stack(new_k, axis=1), jnp.stack(new_v, axis=1), jnp.stack(new_s, axis=1))
```

```python
import functools
import math

import jax
import jax.numpy as jnp
from jax import lax
from jax.experimental import pallas as pl
from jax.experimental.pallas import tpu as pltpu

F32 = jnp.float32
BF16 = jnp.bfloat16
I32 = jnp.int32

D_MODEL = 2048
GRID_W = 64
BLOCK = 128
EPS = 1e-6
NEG_INF = -1e30
N_Q_HEADS = 16
N_KV_HEADS = 4
HEAD_DIM = 64
Q_PER_KV = N_Q_HEADS // N_KV_HEADS
ROPE_BASE = 10000.0
ROPE_FREQS = HEAD_DIM // 4
ATTN_SCALE = HEAD_DIM ** -0.5
ATTN_WIDTH = N_Q_HEADS * HEAD_DIM
KV_WIDTH = N_KV_HEADS * HEAD_DIM
CONV_WIDTH = 1024
CONV_KERNEL = 31
CONV_PAD = CONV_KERNEL // 2
CONV_HALO = 16
SSD_HEADS = 16
SSD_HEAD_DIM = 64
SSD_INNER = SSD_HEADS * SSD_HEAD_DIM
SSD_GROUPS = 2
SSD_HPG = SSD_HEADS // SSD_GROUPS
D_STATE = 128
SSD_CONV = 4
SSD_CHUNK = 128
XBC_WIDTH = SSD_INNER + 2 * SSD_GROUPS * D_STATE
PEER_HEADS = 8
N_KEYS = 128
N_EXPERTS = N_KEYS * N_KEYS
PEER_HALF = 128
PEER_TOPK = 16
PEER_SLOTS = PEER_HEADS * PEER_TOPK
N_BRANCH = 3
LANES = 128

COL_XBC = 0
COL_K = XBC_WIDTH
COL_V = COL_K + KV_WIDTH
COL_CONV = COL_V + KV_WIDTH
COL_GATES = COL_CONV + 2 * CONV_WIDTH
COL_Q = COL_GATES + N_BRANCH * D_MODEL
COL_Z = COL_Q + ATTN_WIDTH
U_WIDTH = COL_Z + SSD_INNER

VMEM_LIMIT = 48 * 1024 * 1024


def _cparams(*sem):
    return pltpu.CompilerParams(dimension_semantics=sem, vmem_limit_bytes=VMEM_LIMIT)


def _mm_kernel(*refs, silu_in, has_bias, has_res):
    x_ref, w_ref = refs[0], refs[1]
    pos = 2
    x = x_ref[...]
    if silu_in:
        x = x * jax.nn.sigmoid(x)
    acc = jnp.dot(x.astype(BF16), w_ref[...].astype(BF16), preferred_element_type=F32)
    if has_bias:
        acc = acc + refs[pos][...]
        pos += 1
    if has_res:
        acc = refs[pos][...] + refs[pos + 1][...] * acc
        pos += 2
    o_ref = refs[pos]
    o_ref[...] = acc.astype(o_ref.dtype)


def _matmul(x, w, *, tm, tn, name, out_dtype=F32, silu_in=False, bias=None, res=None, gate=None, gate_row=None):
    m, k = x.shape
    n = w.shape[1]
    in_specs = [pl.BlockSpec((tm, k), lambda i, j: (i, 0)), pl.BlockSpec((k, tn), lambda i, j: (0, j))]
    args = [x, w]
    if bias is not None:
        in_specs.append(pl.BlockSpec((1, tn), lambda i, j: (0, j)))
        args.append(bias)
    if res is not None:
        in_specs.append(pl.BlockSpec((tm, tn), lambda i, j: (i, j)))
        in_specs.append(pl.BlockSpec((None, 1, tn), lambda i, j: (gate_row(i), 0, j)))
        args += [res, gate]
    return pl.pallas_call(
        functools.partial(_mm_kernel, silu_in=silu_in, has_bias=bias is not None, has_res=res is not None),
        grid=(m // tm, n // tn),
        in_specs=in_specs,
        out_specs=pl.BlockSpec((tm, tn), lambda i, j: (i, j)),
        out_shape=jax.ShapeDtypeStruct((m, n), out_dtype),
        compiler_params=_cparams("parallel", "arbitrary"),
        name=name,
    )(*args)


def _norm_mod_kernel(x_ref, g_ref, shift_ref, scale_ref, *o_refs):
    x = x_ref[...]
    y = x * lax.rsqrt(jnp.mean(x * x, axis=-1, keepdims=True) + EPS)
    h = (y * g_ref[...]) * (1.0 + scale_ref[...]) + shift_ref[...]
    for o_ref in o_refs:
        o_ref[...] = h.astype(o_ref.dtype)


def _norm_mod(x, g, shift, scale, row_of_tile, *, tm, out_dtypes):
    m, d = x.shape
    mod_spec = pl.BlockSpec((None, 1, d), lambda i: (row_of_tile(i), 0, 0))
    outs = pl.pallas_call(
        _norm_mod_kernel,
        grid=(m // tm,),
        in_specs=[pl.BlockSpec((tm, d), lambda i: (i, 0)), pl.BlockSpec((1, d), lambda i: (0, 0)), mod_spec, mod_spec],
        out_specs=[pl.BlockSpec((tm, d), lambda i: (i, 0)) for _ in out_dtypes],
        out_shape=[jax.ShapeDtypeStruct((m, d), dt) for dt in out_dtypes],
        compiler_params=_cparams("parallel"),
        name="norm_mod",
    )(x, g, shift, scale)
    return outs


def _final_norm_kernel(x_ref, g_ref, o_ref):
    x = x_ref[...]
    o_ref[...] = x * lax.rsqrt(jnp.mean(x * x, axis=-1, keepdims=True) + EPS) * g_ref[...]


def _final_norm(x, g, *, tm):
    m, d = x.shape
    return pl.pallas_call(
        _final_norm_kernel,
        grid=(m // tm,),
        in_specs=[pl.BlockSpec((tm, d), lambda i: (i, 0)), pl.BlockSpec((1, d), lambda i: (0, 0))],
        out_specs=pl.BlockSpec((tm, d), lambda i: (i, 0)),
        out_shape=jax.ShapeDtypeStruct((m, d), F32),
        compiler_params=_cparams("parallel"),
        name="final_norm",
    )(x, g)


def _sink_column(sink_ref, kh, rows):
    grp = lax.broadcasted_iota(I32, (rows, 1), 0) // (rows // Q_PER_KV)
    col = jnp.full((rows, 1), sink_ref[kh * Q_PER_KV], F32)
    for g in range(1, Q_PER_KV):
        col = jnp.where(grp == g, sink_ref[kh * Q_PER_KV + g], col)
    return col


def _ctx_attn_kernel(sink_ref, q_ref, k_ref, v_ref, o_ref):
    length = q_ref.shape[0]
    for kh in range(N_KV_HEADS):
        ksl = slice(kh * HEAD_DIM, (kh + 1) * HEAD_DIM)
        kk = k_ref[:, ksl].astype(BF16)
        vv = v_ref[:, ksl].astype(BF16)
        qq = jnp.concatenate(
            [q_ref[:, (kh * Q_PER_KV + g) * HEAD_DIM:(kh * Q_PER_KV + g + 1) * HEAD_DIM] for g in range(Q_PER_KV)],
            axis=0).astype(BF16)
        s = lax.dot_general(qq, kk, (((1,), (1,)), ((), ())), preferred_element_type=F32) * ATTN_SCALE
        sink = _sink_column(sink_ref, kh, Q_PER_KV * length)
        m = jnp.maximum(jnp.max(s, axis=-1, keepdims=True), sink)
        p = jnp.exp(s - m)
        denom = jnp.sum(p, axis=-1, keepdims=True) + jnp.exp(sink - m)
        o = jnp.dot((p / denom).astype(BF16), vv, preferred_element_type=F32)
        for g in range(Q_PER_KV):
            h = kh * Q_PER_KV + g
            o_ref[:, h * HEAD_DIM:(h + 1) * HEAD_DIM] = o[g * length:(g + 1) * length].astype(o_ref.dtype)


def _ctx_attention(u, sink, *, row0, nseq, length):
    r0 = row0 // length
    return pl.pallas_call(
        _ctx_attn_kernel,
        grid=(nseq,),
        in_specs=[
            pl.BlockSpec(memory_space=pltpu.SMEM),
            pl.BlockSpec((length, ATTN_WIDTH), lambda b: (r0 + b, COL_Q // ATTN_WIDTH)),
            pl.BlockSpec((length, KV_WIDTH), lambda b: (r0 + b, COL_K // KV_WIDTH)),
            pl.BlockSpec((length, KV_WIDTH), lambda b: (r0 + b, COL_V // KV_WIDTH)),
        ],
        out_specs=pl.BlockSpec((length, ATTN_WIDTH), lambda b: (b, 0)),
        out_shape=jax.ShapeDtypeStruct((nseq * length, ATTN_WIDTH), BF16),
        compiler_params=_cparams("parallel"),
        name="ctx_attn",
    )(sink, u, u, u)


def _rope_kernel(q_ref, k_ref, cos_ref, sin_ref, qo_ref, ko_ref):
    def rot(x, cos, sin):
        width = x.shape[1]
        lane = lax.broadcasted_iota(I32, x.shape, 1)
        first = (lane // ROPE_FREQS) % 2 == 0
        partner = jnp.where(first, pltpu.roll(x, width - ROPE_FREQS, axis=1), pltpu.roll(x, ROPE_FREQS, axis=1))
        return x * cos + partner * sin

    qo_ref[...] = (rot(q_ref[...], cos_ref[...], sin_ref[...]) * ATTN_SCALE).astype(qo_ref.dtype)
    ko_ref[...] = rot(k_ref[...], cos_ref[:, :KV_WIDTH], sin_ref[:, :KV_WIDTH]).astype(ko_ref.dtype)


def _rope(u, cos, sin, *, row0, nrows, length, tm):
    r0 = row0 // tm
    per_seq = length // tm
    return pl.pallas_call(
        _rope_kernel,
        grid=(nrows // tm,),
        in_specs=[
            pl.BlockSpec((tm, ATTN_WIDTH), lambda i: (r0 + i, COL_Q // ATTN_WIDTH)),
            pl.BlockSpec((tm, KV_WIDTH), lambda i: (r0 + i, COL_K // KV_WIDTH)),
            pl.BlockSpec((tm, ATTN_WIDTH), lambda i: (i % per_seq, 0)),
            pl.BlockSpec((tm, ATTN_WIDTH), lambda i: (i % per_seq, 0)),
        ],
        out_specs=[pl.BlockSpec((tm, ATTN_WIDTH), lambda i: (i, 0)), pl.BlockSpec((tm, KV_WIDTH), lambda i: (i, 0))],
        out_shape=[jax.ShapeDtypeStruct((nrows, ATTN_WIDTH), BF16), jax.ShapeDtypeStruct((nrows, KV_WIDTH), BF16)],
        compiler_params=_cparams("parallel"),
        name="rope",
    )(u, u, cos, sin)


def _lat_attn_kernel(sink_ref, q_ref, kp_ref, kc_ref, kn_ref, vp_ref, vc_ref, vn_ref, ck_ref, cv_ref, o_ref):
    n = pl.program_id(1)
    nb = pl.num_programs(1)
    rows = Q_PER_KV * BLOCK
    r = lax.broadcasted_iota(I32, (rows, 3 * BLOCK), 0) % BLOCK
    c = lax.broadcasted_iota(I32, (rows, 3 * BLOCK), 1)
    lo = jnp.where(n > 0, 0, BLOCK)
    hi = jnp.where(n < nb - 1, 3 * BLOCK, 2 * BLOCK)
    mask = (c >= r) & (c <= r + 2 * BLOCK) & (c >= lo) & (c < hi)
    for kh in range(N_KV_HEADS):
        ksl = slice(kh * HEAD_DIM, (kh + 1) * HEAD_DIM)
        kk = jnp.concatenate([kp_ref[:, ksl], kc_ref[:, ksl], kn_ref[:, ksl], ck_ref[:, ksl].astype(BF16)], axis=0)
        vv = jnp.concatenate([vp_ref[:, ksl], vc_ref[:, ksl], vn_ref[:, ksl], cv_ref[:, ksl]], axis=0).astype(BF16)
        qq = jnp.concatenate(
            [q_ref[:, (kh * Q_PER_KV + g) * HEAD_DIM:(kh * Q_PER_KV + g + 1) * HEAD_DIM] for g in range(Q_PER_KV)],
            axis=0)
        s = lax.dot_general(qq, kk, (((1,), (1,)), ((), ())), preferred_element_type=F32)
        s_win = jnp.where(mask, s[:, :3 * BLOCK], NEG_INF)
        s_ctx = s[:, 3 * BLOCK:]
        sink = _sink_column(sink_ref, kh, rows)
        m = jnp.maximum(jnp.maximum(jnp.max(s_win, axis=-1, keepdims=True), jnp.max(s_ctx, axis=-1, keepdims=True)),
                        sink)
        p_win = jnp.exp(s_win - m)
        p_ctx = jnp.exp(s_ctx - m)
        denom = (jnp.sum(p_win, axis=-1, keepdims=True) + jnp.sum(p_ctx, axis=-1, keepdims=True)
                 + jnp.exp(sink - m))
        inv = 1.0 / denom
        p = jnp.concatenate([p_win * inv, p_ctx * inv], axis=1).astype(BF16)
        o = jnp.dot(p, vv, preferred_element_type=F32)
        for g in range(Q_PER_KV):
            h = kh * Q_PER_KV + g
            o_ref[:, h * HEAD_DIM:(h + 1) * HEAD_DIM] = o[g * BLOCK:(g + 1) * BLOCK].astype(o_ref.dtype)


def _lat_attention(q_rot, k_rot, u, ck, cv, sink, *, row0, nseq, length):
    nb = length // BLOCK
    r0 = row0 // BLOCK

    def prev(b, n):
        return jnp.maximum(n - 1, 0)

    def nxt(b, n):
        return jnp.minimum(n + 1, nb - 1)

    vcol = COL_V // KV_WIDTH
    past = ck.shape[1]
    return pl.pallas_call(
        _lat_attn_kernel,
        grid=(nseq, nb),
        in_specs=[
            pl.BlockSpec(memory_space=pltpu.SMEM),
            pl.BlockSpec((BLOCK, ATTN_WIDTH), lambda b, n: (b * nb + n, 0)),
            pl.BlockSpec((BLOCK, KV_WIDTH), lambda b, n: (b * nb + prev(b, n), 0)),
            pl.BlockSpec((BLOCK, KV_WIDTH), lambda b, n: (b * nb + n, 0)),
            pl.BlockSpec((BLOCK, KV_WIDTH), lambda b, n: (b * nb + nxt(b, n), 0)),
            pl.BlockSpec((BLOCK, KV_WIDTH), lambda b, n: (r0 + b * nb + prev(b, n), vcol)),
            pl.BlockSpec((BLOCK, KV_WIDTH), lambda b, n: (r0 + b * nb + n, vcol)),
            pl.BlockSpec((BLOCK, KV_WIDTH), lambda b, n: (r0 + b * nb + nxt(b, n), vcol)),
            pl.BlockSpec((None, past, KV_WIDTH), lambda b, n: (b, 0, 0)),
            pl.BlockSpec((None, past, KV_WIDTH), lambda b, n: (b, 0, 0)),
        ],
        out_specs=pl.BlockSpec((BLOCK, ATTN_WIDTH), lambda b, n: (b * nb + n, 0)),
        out_shape=jax.ShapeDtypeStruct((nseq * length, ATTN_WIDTH), BF16),
        compiler_params=_cparams("parallel", "arbitrary"),
        name="lat_attn",
    )(sink, q_rot, k_rot, k_rot, k_rot, u, u, u, ck, cv)


def _conv_kernel(prev_ref, cur_ref, next_ref, w_ref, b_ref, lng_ref, lnb_ref, o_ref, hp_ref, *, ctx_blocks, per_seq):
    i = pl.program_id(0)
    tb = cur_ref.shape[0]
    j = (i - ctx_blocks) % per_seq
    is_lat = i >= ctx_blocks
    has_prev = jnp.logical_and(is_lat, j != 0)
    has_next = jnp.logical_and(is_lat, j != per_seq - 1)

    def glu(x):
        return x[:, :CONV_WIDTH] * jax.nn.sigmoid(x[:, CONV_WIDTH:])

    hp_ref[0:CONV_HALO, :] = jnp.where(has_prev, glu(prev_ref[...]), 0.0)
    hp_ref[CONV_HALO:CONV_HALO + tb, :] = glu(cur_ref[...])
    hp_ref[CONV_HALO + tb:2 * CONV_HALO + tb, :] = jnp.where(has_next, glu(next_ref[...]), 0.0)
    acc = jnp.zeros((tb, CONV_WIDTH), F32)
    off = CONV_HALO - CONV_PAD
    for t in range(CONV_KERNEL):
        acc = acc + w_ref[t:t + 1, :] * hp_ref[off + t:off + t + tb, :]
    acc = acc + b_ref[...]
    mu = jnp.mean(acc, axis=-1, keepdims=True)
    xc = acc - mu
    var = jnp.mean(xc * xc, axis=-1, keepdims=True)
    y = xc * lax.rsqrt(var + EPS) * lng_ref[...] + lnb_ref[...]
    o_ref[...] = (y * jax.nn.sigmoid(y)).astype(o_ref.dtype)


def _conformer(u, dw_w, dw_b, ln_g, ln_b, *, ctx_rows, ctx_len, lat_len, tb):
    assert ctx_len == tb
    m = u.shape[0]
    nblk = m // tb
    ctx_blocks = ctx_rows // tb
    per_seq = lat_len // tb
    hb = tb // CONV_HALO
    ccol = COL_CONV // (2 * CONV_WIDTH)
    last = m // CONV_HALO - 1
    return pl.pallas_call(
        functools.partial(_conv_kernel, ctx_blocks=ctx_blocks, per_seq=per_seq),
        grid=(nblk,),
        in_specs=[
            pl.BlockSpec((CONV_HALO, 2 * CONV_WIDTH), lambda i: (jnp.maximum(i * hb - 1, 0), ccol)),
            pl.BlockSpec((tb, 2 * CONV_WIDTH), lambda i: (i, ccol)),
            pl.BlockSpec((CONV_HALO, 2 * CONV_WIDTH), lambda i: (jnp.minimum((i + 1) * hb, last), ccol)),
            pl.BlockSpec((CONV_KERNEL, CONV_WIDTH), lambda i: (0, 0)),
            pl.BlockSpec((1, CONV_WIDTH), lambda i: (0, 0)),
            pl.BlockSpec((1, CONV_WIDTH), lambda i: (0, 0)),
            pl.BlockSpec((1, CONV_WIDTH), lambda i: (0, 0)),
        ],
        out_specs=pl.BlockSpec((tb, CONV_WIDTH), lambda i: (i, 0)),
        out_shape=jax.ShapeDtypeStruct((m, CONV_WIDTH), BF16),
        scratch_shapes=[pltpu.VMEM((tb + 2 * CONV_HALO, CONV_WIDTH), F32)],
        compiler_params=_cparams("parallel"),
        name="conformer",
    )(u, u, u, dw_w, dw_b, ln_g, ln_b)


def _exact_dot(a, b):
    return jnp.dot(a, b, precision=lax.Precision.HIGHEST, preferred_element_type=F32)


def _ssd_kernel(xbc_ref, dt_ref, cw_ref, cb_ref, dtb_ref, a_ref, e_ref, dskip_ref, h0_ref, y_ref, st_ref,
                ext_ref, *, reverse, lane0):
    c = pl.program_id(1)
    L = SSD_CHUNK
    pad = 8
    cur = xbc_ref[...]

    @pl.when(c == 0)
    def _():
        st_ref[...] = h0_ref[...]
        ext_ref[...] = jnp.zeros_like(ext_ref)

    ext_ref[pad:pad + L, :] = cur
    conv = jnp.zeros((L, XBC_WIDTH), F32)
    for j in range(SSD_CONV):
        off = pad + (SSD_CONV - 1 - j) if reverse else pad - (SSD_CONV - 1) + j
        conv = conv + cw_ref[j:j + 1, :] * ext_ref[off:off + L, :]
    conv = conv + cb_ref[...]
    if reverse:
        ext_ref[pad + L:pad + L + pad, :] = cur[0:pad, :]
    else:
        ext_ref[0:pad, :] = cur[L - pad:L, :]
    act = conv * jax.nn.sigmoid(conv)
    xs = act[:, :SSD_INNER]

    dt = jax.nn.softplus(dt_ref[...] + dtb_ref[...])
    da = dt * a_ref[...]
    ti = lax.broadcasted_iota(I32, (L, L), 0)
    si = lax.broadcasted_iota(I32, (L, L), 1)
    tri = (si >= ti) if reverse else (si <= ti)
    acum = _exact_dot(tri.astype(F32), da)
    acum_t = acum.T
    total = acum[0:1, :] if reverse else acum[L - 1:L, :]
    to_end = jnp.exp(total - acum)
    fac = jnp.concatenate([dt, dt * to_end, jnp.exp(acum), jnp.broadcast_to(jnp.exp(total), (pad, LANES))], axis=0)
    fac = _exact_dot(fac, e_ref[...])
    xg = (xs * fac[0:L]).astype(BF16)
    xg_end = (xs * fac[L:2 * L]).astype(BF16)
    eac = fac[2 * L:3 * L]
    dec = fac[3 * L:3 * L + 1]

    y_ref[...] = dskip_ref[...] * xs
    gw = SSD_HPG * SSD_HEAD_DIM
    for g in range(SSD_GROUPS):
        bg = act[:, SSD_INNER + g * D_STATE:SSD_INNER + (g + 1) * D_STATE]
        cg = act[:, SSD_INNER + SSD_GROUPS * D_STATE + g * D_STATE:SSD_INNER + SSD_GROUPS * D_STATE + (g + 1) * D_STATE]
        bgb = bg.astype(BF16)
        cgb = cg.astype(BF16)
        cbm = lax.dot_general(cgb, bgb, (((1,), (1,)), ((), ())), preferred_element_type=F32)
        st = st_ref[g]
        y_off = jnp.dot(cgb, st.astype(BF16), preferred_element_type=F32) * eac[:, g * gw:(g + 1) * gw]
        y_ref[:, g * gw:(g + 1) * gw] += y_off
        st_ref[g] = st * dec[:, g * gw:(g + 1) * gw] + jnp.dot(
            bg.T.astype(BF16), xg_end[:, g * gw:(g + 1) * gw], preferred_element_type=F32)
        for hh in range(SSD_HPG):
            h = g * SSD_HPG + hh
            lane = lane0 + h
            seg = acum[:, lane:lane + 1] - acum_t[lane:lane + 1, :]
            decay = jnp.exp(jnp.where(tri, seg, -jnp.inf))
            mat = (cbm * decay).astype(BF16)
            y_ref[:, h * SSD_HEAD_DIM:(h + 1) * SSD_HEAD_DIM] += jnp.dot(
                mat, xg[:, h * SSD_HEAD_DIM:(h + 1) * SSD_HEAD_DIM], preferred_element_type=F32)


def _ssd_direction(u, udt, conv_w, conv_b, dt_bias, a_neg, expand, dskip, h0, *, reverse, row0, nseq, length):
    nc = length // SSD_CHUNK
    r0 = row0 // SSD_CHUNK

    def chunk(s, c):
        return r0 + s * nc + (nc - 1 - c if reverse else c)

    return pl.pallas_call(
        functools.partial(_ssd_kernel, reverse=reverse, lane0=SSD_HEADS if reverse else 0),
        grid=(nseq, nc),
        in_specs=[
            pl.BlockSpec((SSD_CHUNK, XBC_WIDTH), lambda s, c: (chunk(s, c), COL_XBC // XBC_WIDTH)),
            pl.BlockSpec((SSD_CHUNK, LANES), lambda s, c: (chunk(s, c), 0)),
            pl.BlockSpec((SSD_CONV, XBC_WIDTH), lambda s, c: (0, 0)),
            pl.BlockSpec((1, XBC_WIDTH), lambda s, c: (0, 0)),
            pl.BlockSpec((1, LANES), lambda s, c: (0, 0)),
            pl.BlockSpec((1, LANES), lambda s, c: (0, 0)),
            pl.BlockSpec((LANES, SSD_INNER), lambda s, c: (0, 0)),
            pl.BlockSpec((1, SSD_INNER), lambda s, c: (0, 0)),
            pl.BlockSpec((None, SSD_GROUPS, D_STATE, SSD_HPG * SSD_HEAD_DIM), lambda s, c: (s, 0, 0, 0)),
        ],
        out_specs=[
            pl.BlockSpec((SSD_CHUNK, SSD_INNER), lambda s, c: (chunk(s, c) - r0, 0)),
            pl.BlockSpec((None, SSD_GROUPS, D_STATE, SSD_HPG * SSD_HEAD_DIM), lambda s, c: (s, 0, 0, 0)),
        ],
        out_shape=[
            jax.ShapeDtypeStruct((nseq * length, SSD_INNER), F32),
            jax.ShapeDtypeStruct((nseq, SSD_GROUPS, D_STATE, SSD_HPG * SSD_HEAD_DIM), F32),
        ],
        scratch_shapes=[pltpu.VMEM((SSD_CHUNK + 16, XBC_WIDTH), F32)],
        compiler_params=_cparams("parallel", "arbitrary"),
        name="ssd_bwd" if reverse else "ssd_fwd",
    )(u, udt, conv_w, conv_b, dt_bias, a_neg, expand, dskip, h0)


def _ssd_gate_kernel(yf_ref, yb_ref, z_ref, g_ref, o_ref):
    z = z_ref[...]
    y = (yf_ref[...] + yb_ref[...]) * (z * jax.nn.sigmoid(z))
    o_ref[...] = (y * lax.rsqrt(jnp.mean(y * y, axis=-1, keepdims=True) + EPS) * g_ref[...]).astype(o_ref.dtype)


def _ssd_gate(yf, yb, u, norm_g, *, tm):
    m = yf.shape[0]
    return pl.pallas_call(
        _ssd_gate_kernel,
        grid=(m // tm,),
        in_specs=[
            pl.BlockSpec((tm, SSD_INNER), lambda i: (i, 0)),
            pl.BlockSpec((tm, SSD_INNER), lambda i: (i, 0)),
            pl.BlockSpec((tm, SSD_INNER), lambda i: (i, COL_Z // SSD_INNER)),
            pl.BlockSpec((1, SSD_INNER), lambda i: (0, 0)),
        ],
        out_specs=pl.BlockSpec((tm, SSD_INNER), lambda i: (i, 0)),
        out_shape=jax.ShapeDtypeStruct((m, SSD_INNER), BF16),
        compiler_params=_cparams("parallel"),
        name="ssd_gate",
    )(yf, yb, u, norm_g)


def _merge_kernel(a_ref, c_ref, s_ref, wa_ref, wc_ref, ws_ref, g0_ref, g1_ref, g2_ref, gb_ref, o_ref):
    acc = None
    for b, (x_ref, w_ref, g_ref) in enumerate(((a_ref, wa_ref, g0_ref), (c_ref, wc_ref, g1_ref), (s_ref, ws_ref, g2_ref))):
        gate = jax.nn.sigmoid(g_ref[...] + gb_ref[b:b + 1, :])
        term = gate * jnp.dot(x_ref[...], w_ref[...], preferred_element_type=F32)
        acc = term if acc is None else acc + term
    o_ref[...] = acc.astype(o_ref.dtype)


def _merge(attn, conv, ssd, w_a, w_c, w_s, u, gate_b, *, tm, tn):
    m = attn.shape[0]
    width = attn.shape[1]
    gcol = COL_GATES // tn
    per = D_MODEL // tn
    act_spec = pl.BlockSpec((tm, width), lambda i, j: (i, 0))
    w_spec = pl.BlockSpec((width, tn), lambda i, j: (0, j))
    return pl.pallas_call(
        _merge_kernel,
        grid=(m // tm, D_MODEL // tn),
        in_specs=[act_spec, act_spec, act_spec, w_spec, w_spec, w_spec,
                  pl.BlockSpec((tm, tn), lambda i, j: (i, gcol + j)),
                  pl.BlockSpec((tm, tn), lambda i, j: (i, gcol + per + j)),
                  pl.BlockSpec((tm, tn), lambda i, j: (i, gcol + 2 * per + j)),
                  pl.BlockSpec((N_BRANCH, tn), lambda i, j: (0, j))],
        out_specs=pl.BlockSpec((tm, tn), lambda i, j: (i, j)),
        out_shape=jax.ShapeDtypeStruct((m, D_MODEL), BF16),
        compiler_params=_cparams("parallel", "arbitrary"),
        name="merge",
    )(attn, conv, ssd, w_a, w_c, w_s, u, u, u, gate_b)


def _top16(s, key=None, payload=None):
    if key is None:
        key = lax.broadcasted_iota(I32, s.shape, 0).astype(F32)
    vals, poss, pays = [], [], []
    for _ in range(PEER_TOPK):
        m = jnp.max(s, axis=0, keepdims=True)
        p = jnp.min(jnp.where(s == m, key, float(N_EXPERTS)), axis=0, keepdims=True)
        hit = key == p
        vals.append(m)
        poss.append(p)
        if payload is not None:
            pays.append(jnp.sum(jnp.where(hit, payload, 0.0), axis=0, keepdims=True))
        s = jnp.where(hit, -jnp.inf, s)
    vals = jnp.concatenate(vals, axis=0)
    if payload is not None:
        return vals, jnp.concatenate(pays, axis=0)
    return vals, jnp.concatenate(poss, axis=0)


def _peer_topk_kernel(q_ref, keys_ref, idx_ref, w_ref):
    for h in range(PEER_HEADS):
        halves = []
        for half in range(2):
            j = h * 2 + half
            qh = q_ref[:, j * PEER_HALF:(j + 1) * PEER_HALF].astype(BF16)
            st = lax.dot_general(keys_ref[j], qh, (((1,), (1,)), ((), ())), preferred_element_type=F32)
            halves.append(_top16(st))
        (v1, p1), (v2, p2) = halves
        sub = lax.broadcasted_iota(I32, (8, v1.shape[1]), 0).astype(F32)
        cand, flat, cidx = [], [], []
        for a in range(8):
            cand.append(v1[a:a + 1, :] + v2[0:8, :])
            flat.append(sub + float(a * PEER_TOPK))
            cidx.append(p1[a:a + 1, :] * N_KEYS + p2[0:8, :])
        cand.append(v1[0:1, :] + v2[8:16, :])
        flat.append(sub + 8.0)
        cidx.append(p1[0:1, :] * N_KEYS + p2[8:16, :])
        cand.append(v1[8:16, :] + v2[0:1, :])
        flat.append((sub + 8.0) * PEER_TOPK)
        cidx.append(p1[8:16, :] * N_KEYS + p2[0:1, :])
        top, idx = _top16(jnp.concatenate(cand, axis=0), jnp.concatenate(flat, axis=0), jnp.concatenate(cidx, axis=0))
        e = jnp.exp(top - top[0:1, :])
        w = e / jnp.sum(e, axis=0, keepdims=True)
        idx_ref[h * PEER_TOPK:(h + 1) * PEER_TOPK, :] = idx.astype(I32)
        w_ref[h * PEER_TOPK:(h + 1) * PEER_TOPK, :] = w


def _peer_topk(q, keys, *, tm):
    m = q.shape[0]
    return pl.pallas_call(
        _peer_topk_kernel,
        grid=(m // tm,),
        in_specs=[pl.BlockSpec((tm, q.shape[1]), lambda i: (i, 0)),
                  pl.BlockSpec(keys.shape, lambda i: (0, 0, 0))],
        out_specs=[pl.BlockSpec((PEER_SLOTS, tm), lambda i: (0, i)), pl.BlockSpec((PEER_SLOTS, tm), lambda i: (0, i))],
        out_shape=[jax.ShapeDtypeStruct((PEER_SLOTS, m), I32), jax.ShapeDtypeStruct((PEER_SLOTS, m), F32)],
        compiler_params=_cparams("parallel"),
        name="peer_topk",
    )(q, keys)


PEER_TOKENS = 128
PEER_BUFS = 8
PEER_AHEAD = PEER_BUFS - 1


def _peer_expert_kernel(idx_ref, idx_next_ref, wt_ref, h_ref, x_ref, gate_ref, uv_ref, o_ref, gbuf, sems):
    step = pl.program_id(0)
    last_step = pl.num_programs(0) - 1
    panels = D_MODEL // LANES
    per_piece = PEER_SLOTS // (2 * panels)
    lane = lax.broadcasted_iota(I32, (PEER_SLOTS, PEER_TOKENS), 1)

    def rows_copy(slot):
        return pltpu.make_async_copy(gbuf.at[slot], gbuf.at[slot], sems.at[slot])

    def start_rows(ids_ref, row, slot, lo, hi):
        for j in range(lo, hi):
            pltpu.make_async_copy(uv_ref.at[ids_ref[row, j]], gbuf.at[slot, :, j, :], sems.at[slot]).start(priority=j % 2)

    def token(t, issue):
        slot = t % PEER_BUFS
        rows_copy(slot).wait()
        xb = h_ref[pl.ds(t, 1), :]
        acc = jnp.zeros((PEER_SLOTS, LANES), F32)
        for g in range(panels):
            issue(g * per_piece, (g + 1) * per_piece)
            u_g = lax.bitcast_convert_type(gbuf[slot, g] & jnp.uint32(0xFFFF0000), F32)
            acc = acc + u_g * xb[:, g * LANES:(g + 1) * LANES]
        a = jnp.sum(acc, axis=1, keepdims=True)
        w = jnp.sum(jnp.where(lane == t, wt_ref[...], 0.0), axis=1, keepdims=True)
        coef = w * (0.5 * a * (1.0 + lax.erf(a * (1.0 / math.sqrt(2.0)))))
        out = []
        for g in range(panels):
            issue((panels + g) * per_piece, (panels + g + 1) * per_piece)
            v_g = lax.bitcast_convert_type(gbuf[slot, g] << 16, F32)
            out.append(jnp.sum(coef * v_g, axis=0, keepdims=True))
        out = jnp.concatenate(out, axis=1)
        o_ref[pl.ds(t, 1), :] = x_ref[pl.ds(t, 1), :] + gate_ref[...] * out

    @pl.when(step == 0)
    def _():
        for t in range(PEER_AHEAD):
            start_rows(idx_ref, t, t, 0, PEER_SLOTS)

    def body_same_step(t, carry):
        ahead = t + PEER_AHEAD
        token(t, functools.partial(start_rows, idx_ref, ahead, ahead % PEER_BUFS))
        return carry

    def body_next_step(t, carry):
        ahead = t + PEER_AHEAD
        token(t, functools.partial(start_rows, idx_next_ref, ahead - PEER_TOKENS, ahead % PEER_BUFS))
        return carry

    def body_drain(t, carry):
        token(t, lambda lo, hi: None)
        return carry

    lax.fori_loop(0, PEER_TOKENS - PEER_AHEAD, body_same_step, 0)

    @pl.when(step < last_step)
    def _():
        lax.fori_loop(PEER_TOKENS - PEER_AHEAD, PEER_TOKENS, body_next_step, 0)

    @pl.when(step == last_step)
    def _():
        lax.fori_loop(PEER_TOKENS - PEER_AHEAD, PEER_TOKENS, body_drain, 0)


def _pack_uv(u, v):
    hi = lax.bitcast_convert_type(u.astype(jnp.bfloat16), jnp.uint16).astype(jnp.uint32) << 16
    lo = lax.bitcast_convert_type(v.astype(jnp.bfloat16), jnp.uint16).astype(jnp.uint32)
    return (hi | lo).reshape(u.shape[0], u.shape[1] // LANES, LANES)


def _peer_experts(idx, wt, h, x, gate, gate_row, uv):
    m = h.shape[0]
    steps = m // PEER_TOKENS
    return pl.pallas_call(
        _peer_expert_kernel,
        grid=(steps,),
        in_specs=[
            pl.BlockSpec((PEER_TOKENS, PEER_SLOTS), lambda i: (i, 0), memory_space=pltpu.SMEM),
            pl.BlockSpec((PEER_TOKENS, PEER_SLOTS), lambda i: (jnp.minimum(i + 1, steps - 1), 0),
                         memory_space=pltpu.SMEM),
            pl.BlockSpec((PEER_SLOTS, PEER_TOKENS), lambda i: (0, i)),
            pl.BlockSpec((PEER_TOKENS, D_MODEL), lambda i: (i, 0)),
            pl.BlockSpec((PEER_TOKENS, D_MODEL), lambda i: (i, 0)),
            pl.BlockSpec((None, 1, D_MODEL), lambda i: (gate_row(i), 0, 0)),
            pl.BlockSpec(memory_space=pl.ANY),
        ],
        out_specs=pl.BlockSpec((PEER_TOKENS, D_MODEL), lambda i: (i, 0)),
        out_shape=jax.ShapeDtypeStruct((m, D_MODEL), F32),
        scratch_shapes=[pltpu.VMEM((PEER_BUFS, D_MODEL // LANES, PEER_SLOTS, LANES), jnp.uint32),
                        pltpu.SemaphoreType.DMA((PEER_BUFS,))],
        compiler_params=_cparams("arbitrary"),
        name="peer_experts",
    )(idx, idx, wt, h, x, gate, uv)


def _rope_tables(length):
    rows = length // GRID_W
    row = jnp.repeat(jnp.arange(rows, dtype=F32), GRID_W)
    col = jnp.tile(jnp.arange(GRID_W, dtype=F32), rows)
    freqs = ROPE_BASE ** (-jnp.arange(ROPE_FREQS, dtype=F32) / ROPE_FREQS)
    ar, ac = row[:, None] * freqs, col[:, None] * freqs
    cos = jnp.concatenate([jnp.cos(ar), jnp.cos(ar), jnp.cos(ac), jnp.cos(ac)], axis=-1)
    sin = jnp.concatenate([-jnp.sin(ar), jnp.sin(ar), -jnp.sin(ac), jnp.sin(ac)], axis=-1)
    return jnp.tile(cos, (1, N_Q_HEADS)), jnp.tile(sin, (1, N_Q_HEADS))


def _pack_w_in(w):
    o_q = 0
    o_k = o_q + ATTN_WIDTH
    o_v = o_k + KV_WIDTH
    o_conv = o_v + KV_WIDTH
    o_z = o_conv + 2 * CONV_WIDTH
    o_xbc = o_z + SSD_INNER
    o_dt = o_xbc + XBC_WIDTH
    o_g = o_dt + 2 * SSD_HEADS
    main = jnp.concatenate([w[:, o_xbc:o_dt], w[:, o_k:o_v], w[:, o_v:o_conv], w[:, o_conv:o_z],
                            w[:, o_g:o_g + N_BRANCH * D_MODEL], w[:, o_q:o_k], w[:, o_z:o_xbc]], axis=1).astype(BF16)
    dt = jnp.pad(w[:, o_dt:o_g], ((0, 0), (0, LANES - 2 * SSD_HEADS))).astype(BF16)
    return main, dt


def _lane_pad(v, lane0):
    return jnp.zeros((1, LANES), F32).at[0, lane0:lane0 + v.shape[0]].set(v)


def _state_to_kernel_layout(h):
    n = h.shape[0]
    h = h.reshape(n, SSD_GROUPS, SSD_HPG, SSD_HEAD_DIM, D_STATE)
    return jnp.transpose(h, (0, 1, 4, 2, 3)).reshape(n, SSD_GROUPS, D_STATE, SSD_HPG * SSD_HEAD_DIM)


def _state_from_kernel_layout(h):
    n = h.shape[0]
    h = h.reshape(n, SSD_GROUPS, D_STATE, SSD_HPG, SSD_HEAD_DIM)
    return jnp.transpose(h, (0, 1, 3, 4, 2)).reshape(n, SSD_HEADS, SSD_HEAD_DIM, D_STATE)


def kernel(x_prompt, x_sample, cache_k, cache_v, state_ssd, c, c_ctx, w_ada, b_ada, norm1_g, norm2_g, w_in, gate_b, attn_sink, w_attn_o, conv_dw_w, conv_dw_b, conv_ln_g, conv_ln_b, w_conv_o, ssd_conv_w, ssd_conv_b, ssd_A_log, ssd_dt_bias, ssd_D, ssd_norm_g, w_ssd_o, w_out, peer_w_q, peer_sub_keys, peer_u, peer_v, final_g):
    nctx, ctx_len, d = x_prompt.shape
    nlat, lat_len, _ = x_sample.shape
    depth = w_in.shape[0]
    ctx_rows = nctx * ctx_len
    lat_rows = nlat * lat_len
    rows = ctx_rows + lat_rows
    past = cache_k.shape[2]
    tm = 512
    big = 1024
    assert ctx_rows % big == 0 and lat_len % big == 0 and ctx_rows % PEER_TOKENS == 0

    def mod_row(tile_rows):
        def f(i):
            r = i * tile_rows
            return jnp.where(r < ctx_rows, 0, 1 + (r - ctx_rows) // lat_len)
        return f

    x = jnp.concatenate([x_prompt.reshape(ctx_rows, d), x_sample.reshape(lat_rows, d)], axis=0)

    cond = jnp.concatenate([c_ctx[None], c, jnp.zeros((8 - 1 - nlat, d), F32)], axis=0)
    mods = []
    for l in range(depth):
        mod = _matmul(cond, w_ada[l], tm=8, tn=1024, name="mm_ada", silu_in=True, bias=b_ada[l][None])
        mods.append(jnp.transpose(mod.reshape(8, 6, d), (1, 0, 2)).reshape(6, 8, 1, d))

    cos, sin = _rope_tables(lat_len)
    expand = jnp.repeat(jnp.eye(LANES, SSD_HEADS, dtype=F32), SSD_HEAD_DIM, axis=1)
    expand = (expand, jnp.roll(expand, SSD_HEADS, axis=0))
    h0_ctx = jnp.zeros((nctx, SSD_GROUPS, D_STATE, SSD_HPG * SSD_HEAD_DIM), F32)

    new_k, new_v, new_s = [], [], []
    for l in range(depth):
        shift1, scale1, gate1, shift2, scale2, gate2 = [mods[l][i] for i in range(6)]
        w_main, w_dt = _pack_w_in(w_in[l])

        (h,) = _norm_mod(x, norm1_g[l][None], shift1, scale1, mod_row(tm), tm=tm, out_dtypes=(BF16,))
        u = _matmul(h, w_main, tm=big, tn=big, name="mm_in")
        udt = _matmul(h, w_dt, tm=1024, tn=LANES, name="mm_dt")

        new_k.append(u[:ctx_rows, COL_K:COL_K + KV_WIDTH].reshape(nctx, ctx_len, N_KV_HEADS, HEAD_DIM))
        new_v.append(u[:ctx_rows, COL_V:COL_V + KV_WIDTH].reshape(nctx, ctx_len, N_KV_HEADS, HEAD_DIM))

        a_ctx = _ctx_attention(u, attn_sink[l], row0=0, nseq=nctx, length=ctx_len)
        q_rot, k_rot = _rope(u, cos, sin, row0=ctx_rows, nrows=lat_rows, length=lat_len, tm=tm)
        a_lat = _lat_attention(q_rot, k_rot, u, cache_k[:, l].reshape(nlat, past, KV_WIDTH),
                               cache_v[:, l].reshape(nlat, past, KV_WIDTH), attn_sink[l],
                               row0=ctx_rows, nseq=nlat, length=lat_len)
        attn = jnp.concatenate([a_ctx, a_lat], axis=0)

        conv = _conformer(u, conv_dw_w[l], conv_dw_b[l][None], conv_ln_g[l][None], conv_ln_b[l][None],
                          ctx_rows=ctx_rows, ctx_len=ctx_len, lat_len=lat_len, tb=ctx_len)

        ys, finals = [], []
        for dr in range(2):
            lane0 = dr * SSD_HEADS
            args = (ssd_conv_w[l, dr], ssd_conv_b[l, dr][None], _lane_pad(ssd_dt_bias[l, dr], lane0),
                    _lane_pad(-jnp.exp(ssd_A_log[l, dr]), lane0), expand[dr],
                    jnp.repeat(ssd_D[l, dr], SSD_HEAD_DIM)[None])
            y_c, f_c = _ssd_direction(u, udt, *args, h0_ctx, reverse=dr == 1, row0=0, nseq=nctx, length=ctx_len)
            y_l, _ = _ssd_direction(u, udt, *args, _state_to_kernel_layout(state_ssd[:, l, dr]),
                                    reverse=dr == 1, row0=ctx_rows, nseq=nlat, length=lat_len)
            ys.append(jnp.concatenate([y_c, y_l], axis=0))
            finals.append(_state_from_kernel_layout(f_c))
        new_s.append(jnp.stack(finals, axis=1))
        ssd = _ssd_gate(ys[0], ys[1], u, ssd_norm_g[l][None], tm=tm)

        merged = _merge(attn, conv, ssd, w_attn_o[l].astype(BF16), w_conv_o[l].astype(BF16), w_ssd_o[l].astype(BF16),
                        u, gate_b[l], tm=big, tn=tm)
        x = _matmul(merged, w_out[l].astype(BF16), tm=big, tn=big, name="mm_out", res=x, gate=gate1,
                    gate_row=mod_row(big))

        h2b, h2 = _norm_mod(x, norm2_g[l][None], shift2, scale2, mod_row(tm), tm=tm, out_dtypes=(BF16, F32))
        q = _matmul(h2b, peer_w_q[l].astype(BF16), tm=big, tn=big, name="mm_peer_q")
        keys = peer_sub_keys[l].reshape(2 * PEER_HEADS, N_KEYS, PEER_HALF).astype(BF16)
        idx_t, w_t = _peer_topk(q, keys, tm=256)
        uv = _pack_uv(peer_u[l], peer_v[l])
        x = _peer_experts(idx_t.T, w_t, h2, x, gate2, mod_row(PEER_TOKENS), uv)

    y = _final_norm(x, final_g[None], tm=tm)
    y_prompt = y[:ctx_rows].reshape(nctx, ctx_len, d)
    y_sample = y[ctx_rows:].reshape(nlat, lat_len, d)
    return (y_prompt, y_sample, jnp.stack(new_k, axis=1), jnp.stack(new_v, axis=1), jnp.stack(new_s, axis=1))
```
